```python
import jax, jax.numpy as jnp
from jax import lax
import numpy as np

D_MODEL = 1024
BATCH = 8
SEQ = 4096
DEPTH = 1

HEAD_DIM = 64
N_Q_HEADS = 8
N_KV_HEADS = 2
Q_PER_KV = N_Q_HEADS // N_KV_HEADS
ATTN_WIDTH = N_Q_HEADS * HEAD_DIM
KV_WIDTH = N_KV_HEADS * HEAD_DIM
IDX_HEADS = 4
IDX_DIM = 64
TOPK_MAX = 256
Q_BLOCK = 128
RNN_WIDTH = D_MODEL
RNN_BLOCKS = 16
RNN_BLOCK_DIM = RNN_WIDTH // RNN_BLOCKS
CONV_WIDTH = 4
LRU_C = 8.0
D_FF = 2816
FFN_CONV_WIDTH = 3
EPS = 1e-6

IN_WIDTHS = (ATTN_WIDTH, KV_WIDTH, KV_WIDTH, IDX_HEADS * IDX_DIM, IDX_DIM, IDX_HEADS,
             RNN_WIDTH, RNN_WIDTH, D_MODEL, D_MODEL)
IN_TOTAL = ATTN_WIDTH + 2 * KV_WIDTH + IDX_HEADS * IDX_DIM + IDX_DIM + IDX_HEADS + 2 * RNN_WIDTH + 2 * D_MODEL

kernel_name = "hybrid_dsa_rglru_convffn"


def rms_norm(x, g):
    xf = x.astype(jnp.float32)
    y = xf * lax.rsqrt(jnp.mean(xf * xf, axis=-1, keepdims=True) + EPS)
    return (y * g.astype(jnp.float32)).astype(x.dtype)


def causal_dwconv(x, w, b):
    K = w.shape[0]
    L = x.shape[1]
    xp = jnp.pad(x, ((0, 0), (K - 1, 0), (0, 0)))
    y = b
    for j in range(K):
        y = y + w[j] * xp[:, j:j + L]
    return y


def dsa_attention(q, k, v, qi, ki, wi, top_k):
    B, L = q.shape[0], q.shape[1]
    nblk = L // Q_BLOCK

    def to_blocks(a):
        return jnp.moveaxis(a.reshape((B, nblk, Q_BLOCK) + a.shape[2:]), 1, 0)

    key_pos = jnp.arange(L)
    gather = jax.vmap(lambda src, idx: src[idx])

    def one_block(args):
        blk, q_b, qi_b, wi_b = args
        q_pos = blk * Q_BLOCK + jnp.arange(Q_BLOCK)
        causal = key_pos[None, :] <= q_pos[:, None]
        rel = jax.nn.relu(jnp.einsum('bqhd,bsd->bqhs', qi_b, ki).astype(jnp.float32))
        score = jnp.einsum('bqh,bqhs->bqs', wi_b.astype(jnp.float32), rel)
        score = jnp.where(causal[None], score, -jnp.inf)
        _, sel = lax.top_k(score, top_k)
        valid = sel <= q_pos[None, :, None]
        k_sel = gather(k, sel)
        v_sel = gather(v, sel)
        qg = q_b.reshape(B, Q_BLOCK, N_KV_HEADS, Q_PER_KV, HEAD_DIM)
        logits = jnp.einsum('bqgrd,bqkgd->bqgrk', qg, k_sel).astype(jnp.float32) * (HEAD_DIM ** -0.5)
        logits = jnp.where(valid[:, :, None, None, :], logits, -jnp.inf)
        p = jax.nn.softmax(logits, axis=-1).astype(v.dtype)
        o = jnp.einsum('bqgrk,bqkgd->bqgrd', p, v_sel)
        return o.reshape(B, Q_BLOCK, ATTN_WIDTH)

    out = lax.map(one_block, (jnp.arange(nblk), to_blocks(q), to_blocks(qi), to_blocks(wi)))
    return jnp.moveaxis(out, 0, 1).reshape(B, L, ATTN_WIDTH)


def rg_lru(xc, wa, ba, wx, bx, lam):
    B, L, W = xc.shape
    xb = xc.reshape(B, L, RNN_BLOCKS, RNN_BLOCK_DIM)
    r = jax.nn.sigmoid(jnp.einsum('blnd,nde->blne', xb, wa).reshape(B, L, W) + ba)
    i = jax.nn.sigmoid(jnp.einsum('blnd,nde->blne', xb, wx).reshape(B, L, W) + bx)
    log_a = LRU_C * r.astype(jnp.float32) * jax.nn.log_sigmoid(lam.astype(jnp.float32))
    a = jnp.exp(log_a)
    b = jnp.sqrt(-jnp.expm1(2.0 * log_a)) * (i * xc).astype(jnp.float32)

    def step(h, ab):
        h = ab[0] * h + ab[1]
        return h, h

    _, hs = lax.scan(step, jnp.zeros((B, W), jnp.float32),
                     (jnp.swapaxes(a, 0, 1), jnp.swapaxes(b, 0, 1)))
    return jnp.swapaxes(hs, 0, 1).astype(xc.dtype)


def hybrid_layer(x, top_k, norm1_g, w_in, q_norm_g, k_norm_g, kidx_norm_g,
                 conv_w, conv_b, rg_wa, rg_ba, rg_wx, rg_bx, rg_lambda,
                 w_o_attn, w_o_rnn, w_out, norm2_g, w_up, ffn_conv_w, ffn_conv_b, w_down):
    B, L, _ = x.shape
    xn = rms_norm(x, norm1_g)
    proj = xn @ w_in
    split_points = np.cumsum(IN_WIDTHS)[:-1].tolist()
    q, k, v, qi, ki, wi, rx, rgate, ga, gb = jnp.split(proj, split_points, axis=-1)

    q = rms_norm(q.reshape(B, L, N_Q_HEADS, HEAD_DIM), q_norm_g)
    k = rms_norm(k.reshape(B, L, N_KV_HEADS, HEAD_DIM), k_norm_g)
    v = v.reshape(B, L, N_KV_HEADS, HEAD_DIM)
    qi = qi.reshape(B, L, IDX_HEADS, IDX_DIM) * (IDX_DIM ** -0.5)
    ki = rms_norm(ki, kidx_norm_g)
    wi = wi * (IDX_HEADS ** -0.5)
    attn = dsa_attention(q, k, v, qi, ki, wi, top_k)

    xc = causal_dwconv(rx, conv_w, conv_b)
    rnn = rg_lru(xc, rg_wa, rg_ba, rg_wx, rg_bx, rg_lambda) * jax.nn.gelu(rgate)

    merged = jax.nn.sigmoid(ga) * (attn @ w_o_attn) + jax.nn.sigmoid(gb) * (rnn @ w_o_rnn)
    x = x + merged @ w_out

    u = causal_dwconv(rms_norm(x, norm2_g) @ w_up, ffn_conv_w, ffn_conv_b)
    gate, val = jnp.split(u, [D_FF], axis=-1)
    return x + (jax.nn.silu(gate) * val) @ w_down


def setup_inputs(seed: int = 0) -> dict:
    key = jax.random.key(seed)
    ks = jax.random.split(key, 24)
    f32 = jnp.float32

    def nrm(k, shape, scale):
        return jax.random.normal(k, shape, f32) * scale

    def gain(k, shape):
        return 1.0 + 0.02 * jax.random.normal(k, shape, f32)

    a0 = jax.random.uniform(ks[13], (DEPTH, RNN_WIDTH), f32, 0.9, 0.999)
    s = a0 ** (1.0 / LRU_C)
    rg_lambda = jnp.log(s) - jnp.log1p(-s)
    return {
        "x": jax.random.normal(ks[0], (BATCH, SEQ, D_MODEL), f32),
        "norm1_g": gain(ks[1], (DEPTH, D_MODEL)),
        "w_in": nrm(ks[2], (DEPTH, D_MODEL, IN_TOTAL), D_MODEL ** -0.5),
        "q_norm_g": gain(ks[3], (DEPTH, HEAD_DIM)),
        "k_norm_g": gain(ks[4], (DEPTH, HEAD_DIM)),
        "kidx_norm_g": gain(ks[5], (DEPTH, IDX_DIM)),
        "conv_w": nrm(ks[6], (DEPTH, CONV_WIDTH, RNN_WIDTH), CONV_WIDTH ** -0.5),
        "conv_b": nrm(ks[7], (DEPTH, RNN_WIDTH), 0.01),
        "rg_wa": nrm(ks[8], (DEPTH, RNN_BLOCKS, RNN_BLOCK_DIM, RNN_BLOCK_DIM), RNN_BLOCK_DIM ** -0.5),
        "rg_ba": nrm(ks[9], (DEPTH, RNN_WIDTH), 0.01),
        "rg_wx": nrm(ks[10], (DEPTH, RNN_BLOCKS, RNN_BLOCK_DIM, RNN_BLOCK_DIM), RNN_BLOCK_DIM ** -0.5),
        "rg_bx": nrm(ks[11], (DEPTH, RNN_WIDTH), 0.01),
        "rg_lambda": rg_lambda,
        "w_o_attn": nrm(ks[14], (DEPTH, ATTN_WIDTH, D_MODEL), ATTN_WIDTH ** -0.5),
        "w_o_rnn": nrm(ks[15], (DEPTH, RNN_WIDTH, D_MODEL), RNN_WIDTH ** -0.5),
        "w_out": nrm(ks[16], (DEPTH, D_MODEL, D_MODEL), D_MODEL ** -0.5),
        "norm2_g": gain(ks[17], (DEPTH, D_MODEL)),
        "w_up": nrm(ks[18], (DEPTH, D_MODEL, 2 * D_FF), D_MODEL ** -0.5),
        "ffn_conv_w": nrm(ks[19], (DEPTH, FFN_CONV_WIDTH, 2 * D_FF), FFN_CONV_WIDTH ** -0.5),
        "ffn_conv_b": nrm(ks[20], (DEPTH, 2 * D_FF), 0.01),
        "w_down": nrm(ks[21], (DEPTH, D_FF, D_MODEL), D_FF ** -0.5),
    }


def reference(x, norm1_g, w_in, q_norm_g, k_norm_g, kidx_norm_g, conv_w, conv_b,
              rg_wa, rg_ba, rg_wx, rg_bx, rg_lambda, w_o_attn, w_o_rnn, w_out,
              norm2_g, w_up, ffn_conv_w, ffn_conv_b, w_down):
    L = x.shape[1]
    top_k = min(TOPK_MAX, L // 4)
    h = x
    for l in range(DEPTH):
        h = hybrid_layer(h, top_k, norm1_g[l], w_in[l], q_norm_g[l], k_norm_g[l], kidx_norm_g[l],
                         conv_w[l], conv_b[l], rg_wa[l], rg_ba[l], rg_wx[l], rg_bx[l], rg_lambda[l],
                         w_o_attn[l], w_o_rnn[l], w_out[l], norm2_g[l], w_up[l],
                         ffn_conv_w[l], ffn_conv_b[l], w_down[l])
    return h
```

```python
import functools

import jax
import jax.numpy as jnp
import numpy as np
from jax import lax
from jax.experimental import pallas as pl
from jax.experimental.pallas import tpu as pltpu

F32 = jnp.float32
BF16 = jnp.bfloat16
I32 = jnp.int32

D_MODEL = 1024
HEAD_DIM = 64
N_Q_HEADS = 8
N_KV_HEADS = 2
Q_PER_KV = N_Q_HEADS // N_KV_HEADS
ATTN_WIDTH = N_Q_HEADS * HEAD_DIM
KV_WIDTH = N_KV_HEADS * HEAD_DIM
IDX_HEADS = 4
IDX_DIM = 64
TOPK_MAX = 256
RNN_WIDTH = D_MODEL
RNN_BLOCKS = 16
RNN_BLOCK_DIM = RNN_WIDTH // RNN_BLOCKS
CONV_WIDTH = 4
LRU_C = 8.0
D_FF = 2816
FFN_CONV_WIDTH = 3
EPS = 1e-6

LANES = 128
SUBLANES = 8
VMEM_LIMIT_BYTES = 56 * 1024 * 1024

PROJ_ROWS = 512
ATTN_SLAB = ATTN_WIDTH + 2 * KV_WIDTH + IDX_HEADS * IDX_DIM + LANES
Q_BLOCK = 128
KEY_CHUNK = 512
RNN_ROWS = 256
MERGE_ROWS = 512
FFN_ROWS = 512
FFN_CHUNK = 256

INT_MIN = -2 ** 31
MASK_BIAS = -1e30
M_INIT = -3e38


def _sigmoid(x):
    return 1.0 / (1.0 + jnp.exp(-x))


def _gelu_tanh(x):
    c = np.sqrt(2.0 / np.pi).astype(np.float32)
    return x * (0.5 * (1.0 + jnp.tanh(c * (x + 0.044715 * (x * x * x)))))


def _pair_norm(slab, gain):
    lane = lax.broadcasted_iota(I32, slab.shape, 1)
    lo = lane < HEAD_DIM
    sq = slab * slab
    s_lo = jnp.sum(jnp.where(lo, sq, 0.0), axis=-1, keepdims=True)
    s_hi = jnp.sum(jnp.where(lo, 0.0, sq), axis=-1, keepdims=True)
    inv = jnp.where(lo, lax.rsqrt(s_lo * (1.0 / HEAD_DIM) + EPS), lax.rsqrt(s_hi * (1.0 / HEAD_DIM) + EPS))
    return slab * inv * gain


def _proj_kernel(x_ref, g1_ref, w_ref, qg_ref, kg_ref, kig_ref,
                 q_ref, k_ref, v_ref, qi_ref, kiwi_ref, rx_ref, rgate_ref, ga_ref, gb_ref):
    x = x_ref[...]
    ms = jnp.mean(x * x, axis=-1, keepdims=True)
    xn = (x * lax.rsqrt(ms + EPS) * g1_ref[...]).astype(BF16)

    pa = jnp.dot(xn, w_ref[:, 0:ATTN_SLAB], preferred_element_type=F32)
    qg = qg_ref[...]
    kg = kg_ref[...]
    scale = HEAD_DIM ** -0.5
    for j in range(ATTN_WIDTH // LANES):
        sl = slice(j * LANES, (j + 1) * LANES)
        q_ref[:, sl] = (_pair_norm(pa[:, sl], qg) * scale).astype(BF16)
    off = ATTN_WIDTH
    k_ref[...] = _pair_norm(pa[:, off:off + KV_WIDTH], kg).astype(BF16)
    off += KV_WIDTH
    v_ref[...] = pa[:, off:off + KV_WIDTH].astype(BF16)
    off += KV_WIDTH
    qi_ref[...] = (pa[:, off:off + IDX_HEADS * IDX_DIM] * (IDX_DIM ** -0.5)).astype(BF16)
    off += IDX_HEADS * IDX_DIM
    tail = pa[:, off:off + LANES]
    lane = lax.broadcasted_iota(I32, tail.shape, 1)
    is_ki = lane < IDX_DIM
    s_ki = jnp.sum(jnp.where(is_ki, tail * tail, 0.0), axis=-1, keepdims=True)
    ki_n = tail * lax.rsqrt(s_ki * (1.0 / IDX_DIM) + EPS) * kig_ref[...]
    kiwi_ref[...] = jnp.where(is_ki, ki_n, tail * (IDX_HEADS ** -0.5))

    off = ATTN_SLAB
    for ref in (rx_ref, rgate_ref, ga_ref, gb_ref):
        ref[...] = jnp.dot(xn, w_ref[:, off:off + D_MODEL], preferred_element_type=F32).astype(BF16)
        off += D_MODEL


def _proj(x2, g1, w_all, qg, kg, kig):
    m = x2.shape[0]
    rows = PROJ_ROWS
    const = lambda i: (0, 0)
    row = lambda i: (i, 0)
    n_all = w_all.shape[1]
    outs = [
        (ATTN_WIDTH, BF16), (KV_WIDTH, BF16), (KV_WIDTH, BF16), (IDX_HEADS * IDX_DIM, BF16), (LANES, F32),
        (D_MODEL, BF16), (D_MODEL, BF16), (D_MODEL, BF16), (D_MODEL, BF16),
    ]
    return pl.pallas_call(
        _proj_kernel,
        grid=(m // rows,),
        in_specs=[
            pl.BlockSpec((rows, D_MODEL), row),
            pl.BlockSpec((1, D_MODEL), const),
            pl.BlockSpec((D_MODEL, n_all), const, pipeline_mode=pl.Buffered(1)),
            pl.BlockSpec((1, LANES), const),
            pl.BlockSpec((1, LANES), const),
            pl.BlockSpec((1, LANES), const),
        ],
        out_specs=[pl.BlockSpec((rows, w), row) for w, _ in outs],
        out_shape=[jax.ShapeDtypeStruct((m, w), dt) for w, dt in outs],
        compiler_params=pltpu.CompilerParams(dimension_semantics=("arbitrary",), vmem_limit_bytes=VMEM_LIMIT_BYTES),
        name="proj",
    )(x2, g1, w_all, qg, kg, kig)


def _dsa_kernel(top_k, q_ref, qi_ref, wi_ref, kit_ref, kt_ref, va_ref, o_ref, key_ref, m_ref, acc_ref):
    blk = pl.program_id(1)
    n_chunks = (blk * Q_BLOCK + Q_BLOCK + KEY_CHUNK - 1) // KEY_CHUNK
    row = blk * Q_BLOCK + lax.broadcasted_iota(I32, (Q_BLOCK, KEY_CHUNK), 0)
    lane_groups = KEY_CHUNK // LANES

    qi = qi_ref[0]
    w = wi_ref[0]

    def score_body(c, carry):
        off = pl.multiple_of(c * KEY_CHUNK, KEY_CHUNK)
        kic = kit_ref[0, :, pl.ds(off, KEY_CHUNK)]
        s = jnp.zeros((Q_BLOCK, KEY_CHUNK), F32)
        for h in range(IDX_HEADS):
            d = jnp.dot(qi[:, h * IDX_DIM:(h + 1) * IDX_DIM], kic, preferred_element_type=F32)
            s = s + w[:, IDX_DIM + h:IDX_DIM + h + 1] * jnp.maximum(d, 0.0)
        s = jnp.where(s == 0.0, 0.0, s)
        bits = lax.bitcast_convert_type(s, I32)
        key = bits ^ ((bits >> 31) & 0x7FFFFFFF)
        col = off + lax.broadcasted_iota(I32, (Q_BLOCK, KEY_CHUNK), 1)
        key_ref[:, pl.ds(off, KEY_CHUNK)] = jnp.where(col <= row, key, INT_MIN)
        return carry

    lax.fori_loop(0, n_chunks, score_body, 0)

    def count(compare, level):
        def body(c, cnt):
            off = pl.multiple_of(c * KEY_CHUNK, KEY_CHUNK)
            for j in range(lane_groups):
                kc = key_ref[:, pl.ds(off + j * LANES, LANES)]
                cnt = cnt + jnp.where(compare(kc, level), 1, 0)
            return cnt
        cnt = lax.fori_loop(0, n_chunks, body, jnp.zeros((Q_BLOCK, LANES), I32))
        return jnp.sum(cnt, axis=-1, keepdims=True)

    def bit_body(it, cand):
        trial = cand + jnp.left_shift(jnp.int32(1), 31 - it)
        return jnp.where(count(jnp.greater_equal, trial) >= top_k, trial, cand)

    thr = lax.fori_loop(0, 32, bit_body, jnp.full((Q_BLOCK, LANES), INT_MIN, I32))
    need = (top_k - count(jnp.greater, thr)).astype(F32)

    q = q_ref[0]
    q_groups = []
    for g in range(N_KV_HEADS):
        heads = [q[:, (g * Q_PER_KV + r) * HEAD_DIM:(g * Q_PER_KV + r + 1) * HEAD_DIM] for r in range(Q_PER_KV)]
        q_groups.append(jnp.concatenate(heads, axis=0))

    tri_r = lax.broadcasted_iota(I32, (KEY_CHUNK, KEY_CHUNK), 0)
    tri_c = lax.broadcasted_iota(I32, (KEY_CHUNK, KEY_CHUNK), 1)
    prefix_mat = jnp.where(tri_r <= tri_c, 1.0, 0.0).astype(BF16)

    m_ref[...] = jnp.full(m_ref.shape, M_INIT, F32)
    acc_ref[...] = jnp.zeros(acc_ref.shape, F32)
    thr_c = jnp.broadcast_to(thr[:, 0:1], (Q_BLOCK, KEY_CHUNK))

    def attn_body(c, ties_before):
        off = pl.multiple_of(c * KEY_CHUNK, KEY_CHUNK)
        kc = key_ref[:, pl.ds(off, KEY_CHUNK)]
        tie = jnp.where(kc == thr_c, 1.0, 0.0)
        tie_rank = ties_before + jnp.dot(tie.astype(BF16), prefix_mat, preferred_element_type=F32)
        keep_tie = jnp.where(tie_rank <= need, tie, 0.0)
        keep = jnp.where(kc > thr_c, 1.0, keep_tie)
        keep = jnp.where(kc == INT_MIN, 0.0, keep)
        bias = jnp.where(keep > 0.5, 0.0, MASK_BIAS)
        for g in range(N_KV_HEADS):
            s = jnp.dot(q_groups[g], kt_ref[0, g, :, pl.ds(off, KEY_CHUNK)], preferred_element_type=F32)
            s = (s.reshape(Q_PER_KV, Q_BLOCK, KEY_CHUNK) + bias[None]).reshape(Q_PER_KV * Q_BLOCK, KEY_CHUNK)
            m_old = m_ref[g]
            m_new = jnp.maximum(m_old, jnp.max(s, axis=-1, keepdims=True))
            alpha = jnp.exp(m_old - m_new)
            p = jnp.exp(s - m_new).astype(BF16)
            pv = jnp.dot(p, va_ref[0, g, pl.ds(off, KEY_CHUNK), :], preferred_element_type=F32)
            acc_ref[g] = alpha * acc_ref[g] + pv
            m_ref[g] = m_new
        return ties_before + jnp.sum(tie, axis=-1, keepdims=True)

    lax.fori_loop(0, n_chunks, attn_body, jnp.zeros((Q_BLOCK, 1), F32))

    for g in range(N_KV_HEADS):
        acc = acc_ref[g]
        out = acc[:, 0:HEAD_DIM] / acc[:, HEAD_DIM:HEAD_DIM + 1]
        for r in range(Q_PER_KV):
            h = g * Q_PER_KV + r
            o_ref[0, :, h * HEAD_DIM:(h + 1) * HEAD_DIM] = out[r * Q_BLOCK:(r + 1) * Q_BLOCK, :].astype(o_ref.dtype)


def _dsa(q, qi, wi, kit, kt, va, top_k):
    b, l, _ = q.shape
    blk = lambda bi, i: (bi, i, 0)
    return pl.pallas_call(
        functools.partial(_dsa_kernel, top_k),
        grid=(b, l // Q_BLOCK),
        in_specs=[
            pl.BlockSpec((1, Q_BLOCK, ATTN_WIDTH), blk),
            pl.BlockSpec((1, Q_BLOCK, IDX_HEADS * IDX_DIM), blk),
            pl.BlockSpec((1, Q_BLOCK, LANES), blk),
            pl.BlockSpec((1, IDX_DIM, l), lambda bi, i: (bi, 0, 0)),
            pl.BlockSpec((1, N_KV_HEADS, HEAD_DIM, l), lambda bi, i: (bi, 0, 0, 0)),
            pl.BlockSpec((1, N_KV_HEADS, l, LANES), lambda bi, i: (bi, 0, 0, 0)),
        ],
        out_specs=pl.BlockSpec((1, Q_BLOCK, ATTN_WIDTH), blk),
        out_shape=jax.ShapeDtypeStruct((b, l, ATTN_WIDTH), BF16),
        scratch_shapes=[
            pltpu.VMEM((Q_BLOCK, l), I32),
            pltpu.VMEM((N_KV_HEADS, Q_PER_KV * Q_BLOCK, 1), F32),
            pltpu.VMEM((N_KV_HEADS, Q_PER_KV * Q_BLOCK, LANES), F32),
        ],
        compiler_params=pltpu.CompilerParams(dimension_semantics=("arbitrary", "arbitrary"),
                                             vmem_limit_bytes=VMEM_LIMIT_BYTES),
        name="dsa",
    )(q, qi, wi, kit, kt, va)


def _rglru_kernel(rx_ref, rgate_ref, cw_ref, cb_ref, wg_ref, ba_ref, bx_ref, lam_ref, o_ref, xbuf_ref, h_ref):
    rows = rx_ref.shape[1]
    halo = SUBLANES

    @pl.when(pl.program_id(1) == 0)
    def _():
        xbuf_ref[0:halo, :] = jnp.zeros((halo, RNN_WIDTH), F32)
        h_ref[...] = jnp.zeros(h_ref.shape, F32)

    x = rx_ref[0].astype(F32)
    xbuf_ref[halo:halo + rows, :] = x
    rowi = lax.broadcasted_iota(I32, (rows, LANES), 0)

    for n in range(RNN_WIDTH // LANES):
        sl = slice(n * LANES, (n + 1) * LANES)
        xc = jnp.broadcast_to(cb_ref[:, sl], (rows, LANES))
        for j in range(CONV_WIDTH):
            start = halo - (CONV_WIDTH - 1) + j
            xc = xc + cw_ref[j:j + 1, sl] * xbuf_ref[start:start + rows, sl]
        gates = jnp.dot(xc.astype(BF16), wg_ref[n], preferred_element_type=F32)
        r = _sigmoid(gates[:, 0:LANES] + ba_ref[:, sl])
        i = _sigmoid(gates[:, LANES:2 * LANES] + bx_ref[:, sl])
        lam = lam_ref[:, sl]
        log_sig = jnp.minimum(lam, 0.0) - jnp.log1p(jnp.exp(-jnp.abs(lam)))
        log_a = LRU_C * r * log_sig
        a = jnp.exp(log_a)
        b = jnp.sqrt(1.0 - a * a) * (i * xc)
        d = 1
        while d < rows:
            a_prev = pltpu.roll(a, d, axis=0)
            b_prev = pltpu.roll(b, d, axis=0)
            live = rowi >= d
            b = jnp.where(live, a * b_prev + b, b)
            a = jnp.where(live, a * a_prev, a)
            d *= 2
        h = a * h_ref[:, sl] + b
        h_ref[:, sl] = h[rows - 1:rows, :]
        o_ref[0, :, sl] = (h * _gelu_tanh(rgate_ref[0, :, sl].astype(F32))).astype(o_ref.dtype)

    xbuf_ref[0:halo, :] = x[rows - halo:rows, :]


def _rglru(rx, rgate, cw, cb, wg, ba, bx, lam):
    b, l, _ = rx.shape
    rows = RNN_ROWS
    blk = lambda bi, i: (bi, i, 0)
    c2 = lambda bi, i: (0, 0)
    c3 = lambda bi, i: (0, 0, 0)
    return pl.pallas_call(
        _rglru_kernel,
        grid=(b, l // rows),
        in_specs=[
            pl.BlockSpec((1, rows, RNN_WIDTH), blk),
            pl.BlockSpec((1, rows, RNN_WIDTH), blk),
            pl.BlockSpec((CONV_WIDTH, RNN_WIDTH), c2),
            pl.BlockSpec((1, RNN_WIDTH), c2),
            pl.BlockSpec((RNN_WIDTH // LANES, LANES, 2 * LANES), c3),
            pl.BlockSpec((1, RNN_WIDTH), c2),
            pl.BlockSpec((1, RNN_WIDTH), c2),
            pl.BlockSpec((1, RNN_WIDTH), c2),
        ],
        out_specs=pl.BlockSpec((1, rows, RNN_WIDTH), blk),
        out_shape=jax.ShapeDtypeStruct((b, l, RNN_WIDTH), BF16),
        scratch_shapes=[
            pltpu.VMEM((rows + SUBLANES, RNN_WIDTH), F32),
            pltpu.VMEM((1, RNN_WIDTH), F32),
        ],
        compiler_params=pltpu.CompilerParams(dimension_semantics=("arbitrary", "arbitrary"),
                                             vmem_limit_bytes=VMEM_LIMIT_BYTES),
        name="rglru",
    )(rx, rgate, cw, cb, wg, ba, bx, lam)


def _merge_kernel(attn_ref, rnn_ref, ga_ref, gb_ref, x_ref, woa_ref, wor_ref, wout_ref, g2_ref, x1_ref, xn_ref):
    a = jnp.dot(attn_ref[...], woa_ref[...], preferred_element_type=F32)
    r = jnp.dot(rnn_ref[...], wor_ref[...], preferred_element_type=F32)
    merged = _sigmoid(ga_ref[...].astype(F32)) * a + _sigmoid(gb_ref[...].astype(F32)) * r
    x1 = x_ref[...] + jnp.dot(merged.astype(BF16), wout_ref[...], preferred_element_type=F32)
    x1_ref[...] = x1
    ms = jnp.mean(x1 * x1, axis=-1, keepdims=True)
    xn_ref[...] = (x1 * lax.rsqrt(ms + EPS) * g2_ref[...]).astype(BF16)


def _merge(attn, rnn, ga, gb, x2, woa, wor, wout, g2):
    m = x2.shape[0]
    rows = MERGE_ROWS
    row = lambda i: (i, 0)
    const = lambda i: (0, 0)
    return pl.pallas_call(
        _merge_kernel,
        grid=(m // rows,),
        in_specs=[
            pl.BlockSpec((rows, ATTN_WIDTH), row),
            pl.BlockSpec((rows, RNN_WIDTH), row),
            pl.BlockSpec((rows, D_MODEL), row),
            pl.BlockSpec((rows, D_MODEL), row),
            pl.BlockSpec((rows, D_MODEL), row),
            pl.BlockSpec((ATTN_WIDTH, D_MODEL), const, pipeline_mode=pl.Buffered(1)),
            pl.BlockSpec((RNN_WIDTH, D_MODEL), const, pipeline_mode=pl.Buffered(1)),
            pl.BlockSpec((D_MODEL, D_MODEL), const, pipeline_mode=pl.Buffered(1)),
            pl.BlockSpec((1, D_MODEL), const),
        ],
        out_specs=[pl.BlockSpec((rows, D_MODEL), row), pl.BlockSpec((rows, D_MODEL), row)],
        out_shape=[jax.ShapeDtypeStruct((m, D_MODEL), F32), jax.ShapeDtypeStruct((m, D_MODEL), BF16)],
        compiler_params=pltpu.CompilerParams(dimension_semantics=("arbitrary",), vmem_limit_bytes=VMEM_LIMIT_BYTES),
        name="merge",
    )(attn, rnn, ga, gb, x2, woa, wor, wout, g2)


def _ffn_kernel(xn_ref, x1_ref, wup_ref, cw_ref, cb_ref, wdn_ref, o_ref, tail_ref, acc_ref):
    rows = xn_ref.shape[1]

    @pl.when(pl.program_id(1) == 0)
    def _():
        tail_ref[...] = jnp.zeros(tail_ref.shape, F32)

    xn = xn_ref[0]
    rowi = lax.broadcasted_iota(I32, (rows, FFN_CHUNK), 0)
    first = rowi == 0
    second = rowi == 1

    def conv(col0):
        u = jnp.dot(xn, wup_ref[:, col0:col0 + FFN_CHUNK], preferred_element_type=F32)
        prev1 = tail_ref[SUBLANES - 1:SUBLANES, col0:col0 + FFN_CHUNK]
        prev2 = tail_ref[SUBLANES - 2:SUBLANES - 1, col0:col0 + FFN_CHUNK]
        tail_ref[:, col0:col0 + FFN_CHUNK] = u[rows - SUBLANES:rows, :]
        u1 = jnp.where(first, prev1, pltpu.roll(u, 1, axis=0))
        u2 = jnp.where(first, prev2, jnp.where(second, prev1, pltpu.roll(u, 2, axis=0)))
        cw = cw_ref[:, col0:col0 + FFN_CHUNK]
        return cb_ref[:, col0:col0 + FFN_CHUNK] + cw[0:1] * u2 + cw[1:2] * u1 + cw[2:3] * u

    for j in range(D_FF // FFN_CHUNK):
        gate = conv(j * FFN_CHUNK)
        val = conv(D_FF + j * FFN_CHUNK)
        hid = (gate * _sigmoid(gate) * val).astype(BF16)
        part = jnp.dot(hid, wdn_ref[j * FFN_CHUNK:(j + 1) * FFN_CHUNK, :], preferred_element_type=F32)
        if j == 0:
            acc_ref[...] = part
        else:
            acc_ref[...] += part
    o_ref[0] = x1_ref[0] + acc_ref[...]


def _ffn(xn, x1, wup, cw, cb, wdn):
    b, l, _ = xn.shape
    rows = FFN_ROWS
    blk = lambda bi, i: (bi, i, 0)
    const = lambda bi, i: (0, 0)
    return pl.pallas_call(
        _ffn_kernel,
        grid=(b, l // rows),
        in_specs=[
            pl.BlockSpec((1, rows, D_MODEL), blk),
            pl.BlockSpec((1, rows, D_MODEL), blk),
            pl.BlockSpec((D_MODEL, 2 * D_FF), const, pipeline_mode=pl.Buffered(1)),
            pl.BlockSpec((FFN_CONV_WIDTH, 2 * D_FF), const),
            pl.BlockSpec((1, 2 * D_FF), const),
            pl.BlockSpec((D_FF, D_MODEL), const, pipeline_mode=pl.Buffered(1)),
        ],
        out_specs=pl.BlockSpec((1, rows, D_MODEL), blk),
        out_shape=jax.ShapeDtypeStruct((b, l, D_MODEL), F32),
        scratch_shapes=[
            pltpu.VMEM((SUBLANES, 2 * D_FF), F32),
            pltpu.VMEM((rows, D_MODEL), F32),
        ],
        compiler_params=pltpu.CompilerParams(dimension_semantics=("arbitrary", "arbitrary"),
                                             vmem_limit_bytes=VMEM_LIMIT_BYTES),
        name="ffn",
    )(xn, x1, wup, cw, cb, wdn)


def _block_diag_pairs(w):
    z = jnp.zeros((RNN_BLOCK_DIM, RNN_BLOCK_DIM), w.dtype)
    w = w.reshape(RNN_BLOCKS // 2, 2, RNN_BLOCK_DIM, RNN_BLOCK_DIM)
    top = jnp.concatenate([w[:, 0], jnp.broadcast_to(z, w[:, 0].shape)], axis=-1)
    bot = jnp.concatenate([jnp.broadcast_to(z, w[:, 1].shape), w[:, 1]], axis=-1)
    return jnp.concatenate([top, bot], axis=-2)


def _layer(x, top_k, norm1_g, w_in, q_norm_g, k_norm_g, kidx_norm_g, conv_w, conv_b, rg_wa, rg_ba, rg_wx, rg_bx,
           rg_lambda, w_o_attn, w_o_rnn, w_out, norm2_g, w_up, ffn_conv_w, ffn_conv_b, w_down):
    b, l, _ = x.shape
    m = b * l
    x2 = x.reshape(m, D_MODEL)

    n_head = ATTN_WIDTH + 2 * KV_WIDTH + IDX_HEADS * IDX_DIM + IDX_DIM + IDX_HEADS
    pad = jnp.zeros((D_MODEL, ATTN_SLAB - n_head), w_in.dtype)
    w_all = jnp.concatenate([w_in[:, :n_head], pad, w_in[:, n_head:]], axis=1).astype(BF16)
    pair = lambda g: jnp.tile(g, 2).reshape(1, LANES)
    kig = jnp.concatenate([kidx_norm_g, jnp.zeros((LANES - IDX_DIM,), F32)]).reshape(1, LANES)

    q, k, v, qi, kiwi, rx, rgate, ga, gb = _proj(x2, norm1_g.reshape(1, D_MODEL), w_all,
                                                 pair(q_norm_g), pair(k_norm_g), kig)

    kit = jnp.swapaxes(kiwi[:, :IDX_DIM].astype(BF16).reshape(b, l, IDX_DIM), 1, 2)
    kt = jnp.transpose(k.reshape(b, l, N_KV_HEADS, HEAD_DIM), (0, 2, 3, 1))
    vg = jnp.transpose(v.reshape(b, l, N_KV_HEADS, HEAD_DIM), (0, 2, 1, 3))
    va = jnp.concatenate([vg, jnp.ones_like(vg)], axis=-1)
    attn = _dsa(q.reshape(b, l, ATTN_WIDTH), qi.reshape(b, l, IDX_HEADS * IDX_DIM), kiwi.reshape(b, l, LANES),
                kit, kt, va, top_k)

    wg = jnp.concatenate([_block_diag_pairs(rg_wa), _block_diag_pairs(rg_wx)], axis=-1).astype(BF16)
    rnn = _rglru(rx.reshape(b, l, RNN_WIDTH), rgate.reshape(b, l, RNN_WIDTH), conv_w, conv_b.reshape(1, -1), wg,
                 rg_ba.reshape(1, -1), rg_bx.reshape(1, -1), rg_lambda.reshape(1, -1))

    x1, xn2 = _merge(attn.reshape(m, ATTN_WIDTH), rnn.reshape(m, RNN_WIDTH), ga, gb, x2,
                     w_o_attn.astype(BF16), w_o_rnn.astype(BF16), w_out.astype(BF16), norm2_g.reshape(1, D_MODEL))

    out = _ffn(xn2.reshape(b, l, D_MODEL), x1.reshape(b, l, D_MODEL), w_up.astype(BF16), ffn_conv_w,
               ffn_conv_b.reshape(1, -1), w_down.astype(BF16))
    return out


def kernel(x, norm1_g, w_in, q_norm_g, k_norm_g, kidx_norm_g, conv_w, conv_b, rg_wa, rg_ba, rg_wx, rg_bx, rg_lambda,
           w_o_attn, w_o_rnn, w_out, norm2_g, w_up, ffn_conv_w, ffn_conv_b, w_down):
    l = x.shape[1]
    top_k = min(TOPK_MAX, l // 4)
    h = x
    for d in range(norm1_g.shape[0]):
        h = _layer(h, top_k, norm1_g[d], w_in[d], q_norm_g[d], k_norm_g[d], kidx_norm_g[d], conv_w[d], conv_b[d],
                   rg_wa[d], rg_ba[d], rg_wx[d], rg_bx[d], rg_lambda[d], w_o_attn[d], w_o_rnn[d], w_out[d],
                   norm2_g[d], w_up[d], ffn_conv_w[d], ffn_conv_b[d], w_down[d])
    return h
```

```python
import functools

import jax
import jax.numpy as jnp
import numpy as np
from jax import lax
from jax.experimental import pallas as pl
from jax.experimental.pallas import tpu as pltpu

F32 = jnp.float32
BF16 = jnp.bfloat16
I32 = jnp.int32

D_MODEL = 1024
HEAD_DIM = 64
N_Q_HEADS = 8
N_KV_HEADS = 2
Q_PER_KV = N_Q_HEADS // N_KV_HEADS
ATTN_WIDTH = N_Q_HEADS * HEAD_DIM
KV_WIDTH = N_KV_HEADS * HEAD_DIM
IDX_HEADS = 4
IDX_DIM = 64
TOPK_MAX = 256
RNN_WIDTH = D_MODEL
RNN_BLOCKS = 16
RNN_BLOCK_DIM = RNN_WIDTH // RNN_BLOCKS
CONV_WIDTH = 4
LRU_C = 8.0
D_FF = 2816
FFN_CONV_WIDTH = 3
EPS = 1e-6

LANES = 128
SUBLANES = 8
VMEM_LIMIT_BYTES = 56 * 1024 * 1024

PROJ_ROWS = 512
ATTN_SLAB = ATTN_WIDTH + 2 * KV_WIDTH + IDX_HEADS * IDX_DIM + LANES
Q_BLOCK = 128
KEY_CHUNK = 512
RNN_ROWS = 256
MERGE_ROWS = 512
FFN_ROWS = 512
FFN_CHUNK = 256

INT_MIN = -2 ** 31
MASK_BIAS = -1e30
M_INIT = -3e38


def _sigmoid(x):
    return 1.0 / (1.0 + jnp.exp(-x))


def _gelu_tanh(x):
    c = np.sqrt(2.0 / np.pi).astype(np.float32)
    return x * (0.5 * (1.0 + jnp.tanh(c * (x + 0.044715 * (x * x * x)))))


def _pair_norm(slab, gain):
    lane = lax.broadcasted_iota(I32, slab.shape, 1)
    lo = lane < HEAD_DIM
    sq = slab * slab
    s_lo = jnp.sum(jnp.where(lo, sq, 0.0), axis=-1, keepdims=True)
    s_hi = jnp.sum(jnp.where(lo, 0.0, sq), axis=-1, keepdims=True)
    inv = jnp.where(lo, lax.rsqrt(s_lo * (1.0 / HEAD_DIM) + EPS), lax.rsqrt(s_hi * (1.0 / HEAD_DIM) + EPS))
    return slab * inv * gain


def _proj_kernel(x_ref, g1_ref, w_ref, qg_ref, kg_ref, kig_ref,
                 q_ref, k_ref, v_ref, qi_ref, kiwi_ref, rx_ref, rgate_ref, ga_ref, gb_ref):
    x = x_ref[...]
    ms = jnp.mean(x * x, axis=-1, keepdims=True)
    xn = (x * lax.rsqrt(ms + EPS) * g1_ref[...]).astype(BF16)

    pa = jnp.dot(xn, w_ref[:, 0:ATTN_SLAB], preferred_element_type=F32)
    qg = qg_ref[...]
    kg = kg_ref[...]
    scale = HEAD_DIM ** -0.5
    for j in range(ATTN_WIDTH // LANES):
        sl = slice(j * LANES, (j + 1) * LANES)
        q_ref[:, sl] = (_pair_norm(pa[:, sl], qg) * scale).astype(BF16)
    off = ATTN_WIDTH
    k_ref[...] = _pair_norm(pa[:, off:off + KV_WIDTH], kg).astype(BF16)
    off += KV_WIDTH
    v_ref[...] = pa[:, off:off + KV_WIDTH].astype(BF16)
    off += KV_WIDTH
    qi_ref[...] = (pa[:, off:off + IDX_HEADS * IDX_DIM] * (IDX_DIM ** -0.5)).astype(BF16)
    off += IDX_HEADS * IDX_DIM
    tail = pa[:, off:off + LANES]
    lane = lax.broadcasted_iota(I32, tail.shape, 1)
    is_ki = lane < IDX_DIM
    s_ki = jnp.sum(jnp.where(is_ki, tail * tail, 0.0), axis=-1, keepdims=True)
    ki_n = tail * lax.rsqrt(s_ki * (1.0 / IDX_DIM) + EPS) * kig_ref[...]
    kiwi_ref[...] = jnp.where(is_ki, ki_n, tail * (IDX_HEADS ** -0.5))

    off = ATTN_SLAB
    for ref in (rx_ref, rgate_ref, ga_ref, gb_ref):
        ref[...] = jnp.dot(xn, w_ref[:, off:off + D_MODEL], preferred_element_type=F32).astype(BF16)
        off += D_MODEL


def _proj(x2, g1, w_all, qg, kg, kig):
    m = x2.shape[0]
    rows = PROJ_ROWS
    const = lambda i: (0, 0)
    row = lambda i: (i, 0)
    n_all = w_all.shape[1]
    outs = [
        (ATTN_WIDTH, BF16), (KV_WIDTH, BF16), (KV_WIDTH, BF16), (IDX_HEADS * IDX_DIM, BF16), (LANES, F32),
        (D_MODEL, BF16), (D_MODEL, BF16), (D_MODEL, BF16), (D_MODEL, BF16),
    ]
    return pl.pallas_call(
        _proj_kernel,
        grid=(m // rows,),
        in_specs=[
            pl.BlockSpec((rows, D_MODEL), row),
            pl.BlockSpec((1, D_MODEL), const),
            pl.BlockSpec((D_MODEL, n_all), const, pipeline_mode=pl.Buffered(1)),
            pl.BlockSpec((1, LANES), const),
            pl.BlockSpec((1, LANES), const),
            pl.BlockSpec((1, LANES), const),
        ],
        out_specs=[pl.BlockSpec((rows, w), row) for w, _ in outs],
        out_shape=[jax.ShapeDtypeStruct((m, w), dt) for w, dt in outs],
        compiler_params=pltpu.CompilerParams(dimension_semantics=("arbitrary",), vmem_limit_bytes=VMEM_LIMIT_BYTES),
        name="proj",
    )(x2, g1, w_all, qg, kg, kig)


def _dsa_kernel(top_k, qt_ref, qit_ref, wt_ref, ki_ref, kg_ref, vta_ref, o_ref, key_ref, m_ref, acc_ref):
    blk = pl.program_id(1)
    n_chunks = (blk * Q_BLOCK + Q_BLOCK + KEY_CHUNK - 1) // KEY_CHUNK
    q_pos = blk * Q_BLOCK + lax.broadcasted_iota(I32, (KEY_CHUNK, Q_BLOCK), 1)
    key_iota = lax.broadcasted_iota(I32, (KEY_CHUNK, Q_BLOCK), 0)

    qit = jnp.concatenate([qit_ref[0, h] for h in range(IDX_HEADS)], axis=1)
    wt = wt_ref[0]

    def score_body(c, carry):
        off = pl.multiple_of(c * KEY_CHUNK, KEY_CHUNK)
        d = jnp.dot(ki_ref[0, pl.ds(off, KEY_CHUNK), :], qit, preferred_element_type=F32)
        s = jnp.zeros((KEY_CHUNK, Q_BLOCK), F32)
        for h in range(IDX_HEADS):
            s = s + wt[h:h + 1, :] * jnp.maximum(d[:, h * Q_BLOCK:(h + 1) * Q_BLOCK], 0.0)
        s = jnp.where(s == 0.0, 0.0, s)
        bits = lax.bitcast_convert_type(s, I32)
        key = bits ^ ((bits >> 31) & 0x7FFFFFFF)
        key_ref[pl.ds(off, KEY_CHUNK), :] = jnp.where(off + key_iota <= q_pos, key, INT_MIN)
        return carry

    lax.fori_loop(0, n_chunks, score_body, 0)

    def count(compare, level):
        def body(c, cnt):
            off = pl.multiple_of(c * KEY_CHUNK, KEY_CHUNK)
            kc = key_ref[pl.ds(off, KEY_CHUNK), :].reshape(KEY_CHUNK // SUBLANES, SUBLANES, Q_BLOCK)
            return cnt + jnp.sum(jnp.where(compare(kc, level[None]), 1, 0), axis=0)
        cnt = lax.fori_loop(0, n_chunks, body, jnp.zeros((SUBLANES, Q_BLOCK), I32))
        return jnp.broadcast_to(jnp.sum(cnt, axis=0, keepdims=True), (SUBLANES, Q_BLOCK))

    def bit_body(it, cand):
        trial = cand + jnp.left_shift(jnp.int32(1), 31 - it)
        return jnp.where(count(jnp.greater_equal, trial) >= top_k, trial, cand)

    thr8 = lax.fori_loop(0, 32, bit_body, jnp.full((SUBLANES, Q_BLOCK), INT_MIN, I32))
    need = (top_k - count(jnp.greater, thr8)[0:1, :]).astype(F32)
    thr = thr8[0:1, :]

    qt_groups = [jnp.concatenate([qt_ref[0, g * Q_PER_KV + r] for r in range(Q_PER_KV)], axis=1)
                 for g in range(N_KV_HEADS)]
    tri_r = lax.broadcasted_iota(I32, (KEY_CHUNK, KEY_CHUNK), 0)
    tri_c = lax.broadcasted_iota(I32, (KEY_CHUNK, KEY_CHUNK), 1)
    prefix_mat = jnp.where(tri_c <= tri_r, 1.0, 0.0).astype(BF16)

    m_ref[...] = jnp.full(m_ref.shape, M_INIT, F32)
    acc_ref[...] = jnp.zeros(acc_ref.shape, F32)

    def attn_body(c, ties_before):
        off = pl.multiple_of(c * KEY_CHUNK, KEY_CHUNK)
        kc = key_ref[pl.ds(off, KEY_CHUNK), :]
        tie = jnp.where(kc == thr, 1.0, 0.0)
        tie_rank = ties_before + jnp.dot(prefix_mat, tie.astype(BF16), preferred_element_type=F32)
        keep_tie = jnp.where(tie_rank <= need, tie, 0.0)
        keep = jnp.where(kc > thr, 1.0, keep_tie)
        keep = jnp.where(kc == INT_MIN, 0.0, keep)
        bias = jnp.where(keep > 0.5, 0.0, MASK_BIAS)
        bias4 = jnp.concatenate([bias] * Q_PER_KV, axis=1)
        for g in range(N_KV_HEADS):
            s = jnp.dot(kg_ref[0, g, pl.ds(off, KEY_CHUNK), :], qt_groups[g], preferred_element_type=F32) + bias4
            m_old = m_ref[g]
            m_new = jnp.maximum(m_old, jnp.max(s, axis=0, keepdims=True))
            alpha = jnp.exp(m_old - m_new)
            p = jnp.exp(s - m_new).astype(BF16)
            pv = jnp.dot(vta_ref[0, g, :, pl.ds(off, KEY_CHUNK)], p, preferred_element_type=F32)
            acc_ref[g] = alpha * acc_ref[g] + pv
            m_ref[g] = m_new
        return tie_rank[KEY_CHUNK - 1:KEY_CHUNK, :]

    lax.fori_loop(0, n_chunks, attn_body, jnp.zeros((1, Q_BLOCK), F32))

    for g in range(N_KV_HEADS):
        acc = acc_ref[g]
        out = acc[0:HEAD_DIM, :] / acc[HEAD_DIM:HEAD_DIM + 1, :]
        for r in range(Q_PER_KV):
            h = g * Q_PER_KV + r
            o_ref[0, h * HEAD_DIM:(h + 1) * HEAD_DIM, :] = out[:, r * Q_BLOCK:(r + 1) * Q_BLOCK].astype(o_ref.dtype)


def _dsa(qt, qit, wt, ki, kg, vta, top_k):
    b, _, _, l = qt.shape
    return pl.pallas_call(
        functools.partial(_dsa_kernel, top_k),
        grid=(b, l // Q_BLOCK),
        in_specs=[
            pl.BlockSpec((1, N_Q_HEADS, HEAD_DIM, Q_BLOCK), lambda bi, i: (bi, 0, 0, i)),
            pl.BlockSpec((1, IDX_HEADS, IDX_DIM, Q_BLOCK), lambda bi, i: (bi, 0, 0, i)),
            pl.BlockSpec((1, IDX_HEADS, Q_BLOCK), lambda bi, i: (bi, 0, i)),
            pl.BlockSpec((1, l, IDX_DIM), lambda bi, i: (bi, 0, 0)),
            pl.BlockSpec((1, N_KV_HEADS, l, HEAD_DIM), lambda bi, i: (bi, 0, 0, 0)),
            pl.BlockSpec((1, N_KV_HEADS, 2 * HEAD_DIM, l), lambda bi, i: (bi, 0, 0, 0)),
        ],
        out_specs=pl.BlockSpec((1, ATTN_WIDTH, Q_BLOCK), lambda bi, i: (bi, 0, i)),
        out_shape=jax.ShapeDtypeStruct((b, ATTN_WIDTH, l), BF16),
        scratch_shapes=[
            pltpu.VMEM((l, Q_BLOCK), I32),
            pltpu.VMEM((N_KV_HEADS, 1, Q_PER_KV * Q_BLOCK), F32),
            pltpu.VMEM((N_KV_HEADS, 2 * HEAD_DIM, Q_PER_KV * Q_BLOCK), F32),
        ],
        compiler_params=pltpu.CompilerParams(dimension_semantics=("arbitrary", "arbitrary"),
                                             vmem_limit_bytes=VMEM_LIMIT_BYTES),
        name="dsa",
    )(qt, qit, wt, ki, kg, vta)


def _rglru_kernel(rx_ref, rgate_ref, cw_ref, cb_ref, wg_ref, ba_ref, bx_ref, lam_ref, o_ref, xbuf_ref, h_ref):
    rows = rx_ref.shape[1]
    halo = SUBLANES

    @pl.when(pl.program_id(1) == 0)
    def _():
        xbuf_ref[0:halo, :] = jnp.zeros((halo, RNN_WIDTH), F32)
        h_ref[...] = jnp.zeros(h_ref.shape, F32)

    x = rx_ref[0].astype(F32)
    xbuf_ref[halo:halo + rows, :] = x
    rowi = lax.broadcasted_iota(I32, (rows, LANES), 0)

    for n in range(RNN_WIDTH // LANES):
        sl = slice(n * LANES, (n + 1) * LANES)
        xc = jnp.broadcast_to(cb_ref[:, sl], (rows, LANES))
        for j in range(CONV_WIDTH):
            start = halo - (CONV_WIDTH - 1) + j
            xc = xc + cw_ref[j:j + 1, sl] * xbuf_ref[start:start + rows, sl]
        gates = jnp.dot(xc.astype(BF16), wg_ref[n], preferred_element_type=F32)
        r = _sigmoid(gates[:, 0:LANES] + ba_ref[:, sl])
        i = _sigmoid(gates[:, LANES:2 * LANES] + bx_ref[:, sl])
        lam = lam_ref[:, sl]
        log_sig = jnp.minimum(lam, 0.0) - jnp.log1p(jnp.exp(-jnp.abs(lam)))
        log_a = LRU_C * r * log_sig
        a = jnp.exp(log_a)
        b = jnp.sqrt(1.0 - a * a) * (i * xc)
        d = 1
        while d < rows:
            a_prev = pltpu.roll(a, d, axis=0)
            b_prev = pltpu.roll(b, d, axis=0)
            live = rowi >= d
            b = jnp.where(live, a * b_prev + b, b)
            a = jnp.where(live, a * a_prev, a)
            d *= 2
        h = a * h_ref[:, sl] + b
        h_ref[:, sl] = h[rows - 1:rows, :]
        o_ref[0, :, sl] = (h * _gelu_tanh(rgate_ref[0, :, sl].astype(F32))).astype(o_ref.dtype)

    xbuf_ref[0:halo, :] = x[rows - halo:rows, :]


def _rglru(rx, rgate, cw, cb, wg, ba, bx, lam):
    b, l, _ = rx.shape
    rows = RNN_ROWS
    blk = lambda bi, i: (bi, i, 0)
    c2 = lambda bi, i: (0, 0)
    c3 = lambda bi, i: (0, 0, 0)
    return pl.pallas_call(
        _rglru_kernel,
        grid=(b, l // rows),
        in_specs=[
            pl.BlockSpec((1, rows, RNN_WIDTH), blk),
            pl.BlockSpec((1, rows, RNN_WIDTH), blk),
            pl.BlockSpec((CONV_WIDTH, RNN_WIDTH), c2),
            pl.BlockSpec((1, RNN_WIDTH), c2),
            pl.BlockSpec((RNN_WIDTH // LANES, LANES, 2 * LANES), c3),
            pl.BlockSpec((1, RNN_WIDTH), c2),
            pl.BlockSpec((1, RNN_WIDTH), c2),
            pl.BlockSpec((1, RNN_WIDTH), c2),
        ],
        out_specs=pl.BlockSpec((1, rows, RNN_WIDTH), blk),
        out_shape=jax.ShapeDtypeStruct((b, l, RNN_WIDTH), BF16),
        scratch_shapes=[
            pltpu.VMEM((rows + SUBLANES, RNN_WIDTH), F32),
            pltpu.VMEM((1, RNN_WIDTH), F32),
        ],
        compiler_params=pltpu.CompilerParams(dimension_semantics=("arbitrary", "arbitrary"),
                                             vmem_limit_bytes=VMEM_LIMIT_BYTES),
        name="rglru",
    )(rx, rgate, cw, cb, wg, ba, bx, lam)


def _merge_kernel(attn_ref, rnn_ref, ga_ref, gb_ref, x_ref, woa_ref, wor_ref, wout_ref, g2_ref, x1_ref, xn_ref):
    a = jnp.dot(attn_ref[...], woa_ref[...], preferred_element_type=F32)
    r = jnp.dot(rnn_ref[...], wor_ref[...], preferred_element_type=F32)
    merged = _sigmoid(ga_ref[...].astype(F32)) * a + _sigmoid(gb_ref[...].astype(F32)) * r
    x1 = x_ref[...] + jnp.dot(merged.astype(BF16), wout_ref[...], preferred_element_type=F32)
    x1_ref[...] = x1
    ms = jnp.mean(x1 * x1, axis=-1, keepdims=True)
    xn_ref[...] = (x1 * lax.rsqrt(ms + EPS) * g2_ref[...]).astype(BF16)


def _merge(attn, rnn, ga, gb, x2, woa, wor, wout, g2):
    m = x2.shape[0]
    rows = MERGE_ROWS
    row = lambda i: (i, 0)
    const = lambda i: (0, 0)
    return pl.pallas_call(
        _merge_kernel,
        grid=(m // rows,),
        in_specs=[
            pl.BlockSpec((rows, ATTN_WIDTH), row),
            pl.BlockSpec((rows, RNN_WIDTH), row),
            pl.BlockSpec((rows, D_MODEL), row),
            pl.BlockSpec((rows, D_MODEL), row),
            pl.BlockSpec((rows, D_MODEL), row),
            pl.BlockSpec((ATTN_WIDTH, D_MODEL), const, pipeline_mode=pl.Buffered(1)),
            pl.BlockSpec((RNN_WIDTH, D_MODEL), const, pipeline_mode=pl.Buffered(1)),
            pl.BlockSpec((D_MODEL, D_MODEL), const, pipeline_mode=pl.Buffered(1)),
            pl.BlockSpec((1, D_MODEL), const),
        ],
        out_specs=[pl.BlockSpec((rows, D_MODEL), row), pl.BlockSpec((rows, D_MODEL), row)],
        out_shape=[jax.ShapeDtypeStruct((m, D_MODEL), F32), jax.ShapeDtypeStruct((m, D_MODEL), BF16)],
        compiler_params=pltpu.CompilerParams(dimension_semantics=("arbitrary",), vmem_limit_bytes=VMEM_LIMIT_BYTES),
        name="merge",
    )(attn, rnn, ga, gb, x2, woa, wor, wout, g2)


def _ffn_kernel(xn_ref, x1_ref, wup_ref, cw_ref, cb_ref, wdn_ref, o_ref, tail_ref, acc_ref):
    rows = xn_ref.shape[1]

    @pl.when(pl.program_id(1) == 0)
    def _():
        tail_ref[...] = jnp.zeros(tail_ref.shape, F32)

    xn = xn_ref[0]
    rowi = lax.broadcasted_iota(I32, (rows, FFN_CHUNK), 0)
    first = rowi == 0
    second = rowi == 1

    def conv(col0):
        u = jnp.dot(xn, wup_ref[:, col0:col0 + FFN_CHUNK], preferred_element_type=F32)
        prev1 = tail_ref[SUBLANES - 1:SUBLANES, col0:col0 + FFN_CHUNK]
        prev2 = tail_ref[SUBLANES - 2:SUBLANES - 1, col0:col0 + FFN_CHUNK]
        tail_ref[:, col0:col0 + FFN_CHUNK] = u[rows - SUBLANES:rows, :]
        u1 = jnp.where(first, prev1, pltpu.roll(u, 1, axis=0))
        u2 = jnp.where(first, prev2, jnp.where(second, prev1, pltpu.roll(u, 2, axis=0)))
        cw = cw_ref[:, col0:col0 + FFN_CHUNK]
        return cb_ref[:, col0:col0 + FFN_CHUNK] + cw[0:1] * u2 + cw[1:2] * u1 + cw[2:3] * u

    for j in range(D_FF // FFN_CHUNK):
        gate = conv(j * FFN_CHUNK)
        val = conv(D_FF + j * FFN_CHUNK)
        hid = (gate * _sigmoid(gate) * val).astype(BF16)
        part = jnp.dot(hid, wdn_ref[j * FFN_CHUNK:(j + 1) * FFN_CHUNK, :], preferred_element_type=F32)
        if j == 0:
            acc_ref[...] = part
        else:
            acc_ref[...] += part
    o_ref[0] = x1_ref[0] + acc_ref[...]


def _ffn(xn, x1, wup, cw, cb, wdn):
    b, l, _ = xn.shape
    rows = FFN_ROWS
    blk = lambda bi, i: (bi, i, 0)
    const = lambda bi, i: (0, 0)
    return pl.pallas_call(
        _ffn_kernel,
        grid=(b, l // rows),
        in_specs=[
            pl.BlockSpec((1, rows, D_MODEL), blk),
            pl.BlockSpec((1, rows, D_MODEL), blk),
            pl.BlockSpec((D_MODEL, 2 * D_FF), const, pipeline_mode=pl.Buffered(1)),
            pl.BlockSpec((FFN_CONV_WIDTH, 2 * D_FF), const),
            pl.BlockSpec((1, 2 * D_FF), const),
            pl.BlockSpec((D_FF, D_MODEL), const, pipeline_mode=pl.Buffered(1)),
        ],
        out_specs=pl.BlockSpec((1, rows, D_MODEL), blk),
        out_shape=jax.ShapeDtypeStruct((b, l, D_MODEL), F32),
        scratch_shapes=[
            pltpu.VMEM((SUBLANES, 2 * D_FF), F32),
            pltpu.VMEM((rows, D_MODEL), F32),
        ],
        compiler_params=pltpu.CompilerParams(dimension_semantics=("arbitrary", "arbitrary"),
                                             vmem_limit_bytes=VMEM_LIMIT_BYTES),
        name="ffn",
    )(xn, x1, wup, cw, cb, wdn)


def _block_diag_pairs(w):
    z = jnp.zeros((RNN_BLOCK_DIM, RNN_BLOCK_DIM), w.dtype)
    w = w.reshape(RNN_BLOCKS // 2, 2, RNN_BLOCK_DIM, RNN_BLOCK_DIM)
    top = jnp.concatenate([w[:, 0], jnp.broadcast_to(z, w[:, 0].shape)], axis=-1)
    bot = jnp.concatenate([jnp.broadcast_to(z, w[:, 1].shape), w[:, 1]], axis=-1)
    return jnp.concatenate([top, bot], axis=-2)


def _layer(x, top_k, norm1_g, w_in, q_norm_g, k_norm_g, kidx_norm_g, conv_w, conv_b, rg_wa, rg_ba, rg_wx, rg_bx,
           rg_lambda, w_o_attn, w_o_rnn, w_out, norm2_g, w_up, ffn_conv_w, ffn_conv_b, w_down):
    b, l, _ = x.shape
    m = b * l
    x2 = x.reshape(m, D_MODEL)

    n_head = ATTN_WIDTH + 2 * KV_WIDTH + IDX_HEADS * IDX_DIM + IDX_DIM + IDX_HEADS
    pad = jnp.zeros((D_MODEL, ATTN_SLAB - n_head), w_in.dtype)
    w_all = jnp.concatenate([w_in[:, :n_head], pad, w_in[:, n_head:]], axis=1).astype(BF16)
    pair = lambda g: jnp.tile(g, 2).reshape(1, LANES)
    kig = jnp.concatenate([kidx_norm_g, jnp.zeros((LANES - IDX_DIM,), F32)]).reshape(1, LANES)

    q, k, v, qi, kiwi, rx, rgate, ga, gb = _proj(x2, norm1_g.reshape(1, D_MODEL), w_all,
                                                 pair(q_norm_g), pair(k_norm_g), kig)

    heads_minor = lambda a, n: jnp.transpose(a.reshape(b, l, n, a.shape[-1] // n), (0, 2, 3, 1))
    kiwi3 = kiwi.reshape(b, l, LANES)
    qt = heads_minor(q, N_Q_HEADS)
    qit = heads_minor(qi, IDX_HEADS)
    wt = jnp.swapaxes(kiwi3[:, :, IDX_DIM:IDX_DIM + IDX_HEADS], 1, 2)
    ki = kiwi3[:, :, :IDX_DIM].astype(BF16)
    kg = jnp.transpose(k.reshape(b, l, N_KV_HEADS, HEAD_DIM), (0, 2, 1, 3))
    vt = heads_minor(v, N_KV_HEADS)
    vta = jnp.concatenate([vt, jnp.ones_like(vt)], axis=2)
    attn_t = _dsa(qt, qit, wt, ki, kg, vta, top_k)
    attn = jnp.swapaxes(attn_t, 1, 2)

    wg = jnp.concatenate([_block_diag_pairs(rg_wa), _block_diag_pairs(rg_wx)], axis=-1).astype(BF16)
    rnn = _rglru(rx.reshape(b, l, RNN_WIDTH), rgate.reshape(b, l, RNN_WIDTH), conv_w, conv_b.reshape(1, -1), wg,
                 rg_ba.reshape(1, -1), rg_bx.reshape(1, -1), rg_lambda.reshape(1, -1))

    x1, xn2 = _merge(attn.reshape(m, ATTN_WIDTH), rnn.reshape(m, RNN_WIDTH), ga, gb, x2,
                     w_o_attn.astype(BF16), w_o_rnn.astype(BF16), w_out.astype(BF16), norm2_g.reshape(1, D_MODEL))

    out = _ffn(xn2.reshape(b, l, D_MODEL), x1.reshape(b, l, D_MODEL), w_up.astype(BF16), ffn_conv_w,
               ffn_conv_b.reshape(1, -1), w_down.astype(BF16))
    return out


def kernel(x, norm1_g, w_in, q_norm_g, k_norm_g, kidx_norm_g, conv_w, conv_b, rg_wa, rg_ba, rg_wx, rg_bx, rg_lambda,
           w_o_attn, w_o_rnn, w_out, norm2_g, w_up, ffn_conv_w, ffn_conv_b, w_down):
    l = x.shape[1]
    top_k = min(TOPK_MAX, l // 4)
    h = x
    for d in range(norm1_g.shape[0]):
        h = _layer(h, top_k, norm1_g[d], w_in[d], q_norm_g[d], k_norm_g[d], kidx_norm_g[d], conv_w[d], conv_b[d],
                   rg_wa[d], rg_ba[d], rg_wx[d], rg_bx[d], rg_lambda[d], w_o_attn[d], w_o_rnn[d], w_out[d],
                   norm2_g[d], w_up[d], ffn_conv_w[d], ffn_conv_b[d], w_down[d])
    return h
```

```python
import functools

import jax
import jax.numpy as jnp
import numpy as np
from jax import lax
from jax.experimental import pallas as pl
from jax.experimental.pallas import tpu as pltpu

F32 = jnp.float32
BF16 = jnp.bfloat16
I32 = jnp.int32
I16 = jnp.int16

D_MODEL = 1024
HEAD_DIM = 64
N_Q_HEADS = 8
N_KV_HEADS = 2
Q_PER_KV = N_Q_HEADS // N_KV_HEADS
ATTN_WIDTH = N_Q_HEADS * HEAD_DIM
KV_WIDTH = N_KV_HEADS * HEAD_DIM
IDX_HEADS = 4
IDX_DIM = 64
TOPK_MAX = 256
RNN_WIDTH = D_MODEL
RNN_BLOCKS = 16
RNN_BLOCK_DIM = RNN_WIDTH // RNN_BLOCKS
CONV_WIDTH = 4
LRU_C = 8.0
D_FF = 2816
FFN_CONV_WIDTH = 3
EPS = 1e-6

LANES = 128
SUBLANES = 8
PACK = 16
VMEM_LIMIT_BYTES = 56 * 1024 * 1024

PROJ_ROWS = 512
ATTN_SLAB = ATTN_WIDTH + 2 * KV_WIDTH + IDX_HEADS * IDX_DIM + LANES
Q_BLOCK = 128
KEY_CHUNK = 512
RNN_ROWS = 256
MERGE_ROWS = 512
FFN_ROWS = 512
FFN_CHUNK = 256

ATTN_PIECES = 4
ONES_ROWS = 16
V_ROWS = HEAD_DIM + ONES_ROWS

INT_MIN = -2 ** 31
HALF = 2 ** 15
LOG2E = 1.4426950408889634
MASK_BIAS = -1e30
M_INIT = -3e38


def _sigmoid(x):
    return 1.0 / (1.0 + jnp.exp(-x))


def _gelu_tanh(x):
    c = np.sqrt(2.0 / np.pi).astype(np.float32)
    return x * (0.5 * (1.0 + jnp.tanh(c * (x + 0.044715 * (x * x * x)))))


def _pair_norm(slab, gain):
    lane = lax.broadcasted_iota(I32, slab.shape, 1)
    lo = lane < HEAD_DIM
    sq = slab * slab
    s_lo = jnp.sum(jnp.where(lo, sq, 0.0), axis=-1, keepdims=True)
    s_hi = jnp.sum(jnp.where(lo, 0.0, sq), axis=-1, keepdims=True)
    inv = jnp.where(lo, lax.rsqrt(s_lo * (1.0 / HEAD_DIM) + EPS), lax.rsqrt(s_hi * (1.0 / HEAD_DIM) + EPS))
    return slab * inv * gain


def _proj_kernel(x_ref, g1_ref, w_ref, qg_ref, kg_ref, kig_ref,
                 qt_ref, qit_ref, wt_ref, kib_ref, k_ref, vta_ref, rx_ref, rgate_ref, ga_ref, gb_ref):
    x = x_ref[...]
    rows = x.shape[0]
    ms = jnp.mean(x * x, axis=-1, keepdims=True)
    xn = (x * lax.rsqrt(ms + EPS) * g1_ref[...]).astype(BF16)

    pa = jnp.dot(xn, w_ref[:, 0:ATTN_SLAB], preferred_element_type=F32)
    qg = qg_ref[...]
    kg = kg_ref[...]

    def put_heads(ref, first_head, slab):
        t = slab.T
        ref[0, first_head] = t[0:HEAD_DIM].astype(ref.dtype)
        ref[0, first_head + 1] = t[HEAD_DIM:2 * HEAD_DIM].astype(ref.dtype)

    scale = (HEAD_DIM ** -0.5) * LOG2E
    for j in range(ATTN_WIDTH // LANES):
        sl = slice(j * LANES, (j + 1) * LANES)
        put_heads(qt_ref, 2 * j, _pair_norm(pa[:, sl], qg) * scale)
    off = ATTN_WIDTH
    k_ref[0] = _pair_norm(pa[:, off:off + KV_WIDTH], kg).astype(BF16)
    off += KV_WIDTH
    put_heads(vta_ref.at[:, :, 0:HEAD_DIM, :], 0, pa[:, off:off + KV_WIDTH])
    ones = jnp.ones((ONES_ROWS, rows), BF16)
    for g in range(N_KV_HEADS):
        vta_ref[0, g, HEAD_DIM:HEAD_DIM + ONES_ROWS, :] = ones
    off += KV_WIDTH
    for j in range(IDX_HEADS * IDX_DIM // LANES):
        put_heads(qit_ref, 2 * j, pa[:, off + j * LANES:off + (j + 1) * LANES] * (IDX_DIM ** -0.5))
    off += IDX_HEADS * IDX_DIM
    tail = pa[:, off:off + LANES]
    lane = lax.broadcasted_iota(I32, tail.shape, 1)
    is_ki = lane < IDX_DIM
    s_ki = jnp.sum(jnp.where(is_ki, tail * tail, 0.0), axis=-1, keepdims=True)
    ki_n = tail * lax.rsqrt(s_ki * (1.0 / IDX_DIM) + EPS) * kig_ref[...]
    kiwi = jnp.where(is_ki, ki_n, tail * (IDX_HEADS ** -0.5))
    kib_ref[0] = kiwi.astype(BF16)
    wt_ref[0] = kiwi.T[IDX_DIM:IDX_DIM + SUBLANES]

    off = ATTN_SLAB
    for ref in (rx_ref, rgate_ref, ga_ref, gb_ref):
        ref[...] = jnp.dot(xn, w_ref[:, off:off + D_MODEL], preferred_element_type=F32).astype(BF16)
        off += D_MODEL


def _proj(x2, g1, w_all, qg, kg, kig, b, l):
    m = x2.shape[0]
    rows = PROJ_ROWS
    tpb = l // rows
    const = lambda i: (0, 0)
    row = lambda i: (i, 0)
    n_all = w_all.shape[1]
    head_major = lambda i: (i // tpb, 0, 0, i % tpb)
    seq_major = lambda i: (i // tpb, i % tpb, 0)
    wide = [(D_MODEL, BF16)] * 4
    return pl.pallas_call(
        _proj_kernel,
        grid=(m // rows,),
        in_specs=[
            pl.BlockSpec((rows, D_MODEL), row),
            pl.BlockSpec((1, D_MODEL), const),
            pl.BlockSpec((D_MODEL, n_all), const, pipeline_mode=pl.Buffered(1)),
            pl.BlockSpec((1, LANES), const),
            pl.BlockSpec((1, LANES), const),
            pl.BlockSpec((1, LANES), const),
        ],
        out_specs=[
            pl.BlockSpec((1, N_Q_HEADS, HEAD_DIM, rows), head_major),
            pl.BlockSpec((1, IDX_HEADS, IDX_DIM, rows), head_major),
            pl.BlockSpec((1, SUBLANES, rows), lambda i: (i // tpb, 0, i % tpb)),
            pl.BlockSpec((1, rows, LANES), seq_major),
            pl.BlockSpec((1, rows, LANES), seq_major),
            pl.BlockSpec((1, N_KV_HEADS, V_ROWS, rows), head_major),
        ] + [pl.BlockSpec((rows, w), row) for w, _ in wide],
        out_shape=[
            jax.ShapeDtypeStruct((b, N_Q_HEADS, HEAD_DIM, l), BF16),
            jax.ShapeDtypeStruct((b, IDX_HEADS, IDX_DIM, l), BF16),
            jax.ShapeDtypeStruct((b, SUBLANES, l), F32),
            jax.ShapeDtypeStruct((b, l, LANES), BF16),
            jax.ShapeDtypeStruct((b, l, LANES), BF16),
            jax.ShapeDtypeStruct((b, N_KV_HEADS, V_ROWS, l), BF16),
        ] + [jax.ShapeDtypeStruct((m, w), dt) for w, dt in wide],
        compiler_params=pltpu.CompilerParams(dimension_semantics=("arbitrary",), vmem_limit_bytes=VMEM_LIMIT_BYTES),
        name="proj",
    )(x2, g1, w_all, qg, kg, kig)


def _dsa_kernel(top_k, qt_ref, qit_ref, wt_ref, kib_ref, k_ref, vta_ref, o_ref,
                key_ref, hi_ref, lo_ref, bias_ref, s_ref, p_ref, m_ref, acc_ref):
    blk = pl.program_id(1)
    n_chunks = (blk * Q_BLOCK + Q_BLOCK + KEY_CHUNK - 1) // KEY_CHUNK
    q_pos = blk * Q_BLOCK + lax.broadcasted_iota(I32, (KEY_CHUNK, Q_BLOCK), 1)
    key_iota = lax.broadcasted_iota(I32, (KEY_CHUNK, Q_BLOCK), 0)
    packed = KEY_CHUNK // PACK
    wide = Q_PER_KV * Q_BLOCK

    qit = jnp.concatenate([qit_ref[0, h] for h in range(IDX_HEADS)], axis=1)
    qit = jnp.concatenate([qit, jnp.zeros_like(qit)], axis=0)
    wt = wt_ref[0]

    def score_body(c, carry):
        off = pl.multiple_of(c * KEY_CHUNK, KEY_CHUNK)
        d = jnp.dot(kib_ref[0, pl.ds(off, KEY_CHUNK), :], qit, preferred_element_type=F32)
        s = jnp.zeros((KEY_CHUNK, Q_BLOCK), F32)
        for h in range(IDX_HEADS):
            s = s + wt[h:h + 1, :] * jnp.maximum(d[:, h * Q_BLOCK:(h + 1) * Q_BLOCK], 0.0)
        s = jnp.where(s == 0.0, 0.0, s)
        bits = lax.bitcast_convert_type(s, I32)
        key = bits ^ ((bits >> 31) & 0x7FFFFFFF)
        key = jnp.where(off + key_iota <= q_pos, key, INT_MIN)
        key_ref[pl.ds(off, KEY_CHUNK), :] = key
        base = pl.multiple_of(c * packed, packed)
        hi_ref[pl.ds(base, packed)] = (key >> 16).astype(I16).reshape(packed, PACK, Q_BLOCK)
        lo_ref[pl.ds(base, packed)] = ((key & 0xFFFF) - HALF).astype(I16).reshape(packed, PACK, Q_BLOCK)
        return carry

    lax.fori_loop(0, n_chunks, score_body, 0)

    one = jnp.ones((PACK, Q_BLOCK), BF16)
    zero = jnp.zeros((PACK, Q_BLOCK), BF16)
    n_acc = 4

    def count16(ref, compare, level):
        def body(c, accs):
            base = pl.multiple_of(c * packed, packed)
            accs = list(accs)
            for i in range(packed):
                accs[i % n_acc] = accs[i % n_acc] + jnp.where(compare(ref[base + i], level), one, zero)
            return tuple(accs)
        accs = lax.fori_loop(0, n_chunks, body, (zero,) * n_acc)
        tot = (accs[0].astype(F32) + accs[1].astype(F32)) + (accs[2].astype(F32) + accs[3].astype(F32))
        return jnp.sum(tot, axis=0, keepdims=True)

    def select16(ref, target):
        def bit_body(it, cand):
            trial = cand + jnp.left_shift(jnp.int32(1), 15 - it)
            enough = count16(ref, jnp.greater_equal, trial.astype(I16)) >= target
            return jnp.where(enough, trial, cand)
        return lax.fori_loop(0, 16, bit_body, jnp.full((PACK, Q_BLOCK), -HALF, I32))

    k_f = jnp.full((1, Q_BLOCK), float(top_k), F32)
    hi_thr = select16(hi_ref, k_f)
    hi_thr16 = hi_thr.astype(I16)
    n_above = count16(hi_ref, jnp.greater, hi_thr16)

    def restrict_body(c, carry):
        base = pl.multiple_of(c * packed, packed)
        for i in range(packed):
            lo_ref[base + i] = jnp.where(hi_ref[base + i] == hi_thr16, lo_ref[base + i], jnp.int16(-HALF))
        return carry

    lax.fori_loop(0, n_chunks, restrict_body, 0)
    lo_thr = select16(lo_ref, k_f - n_above)
    n_gt = n_above + count16(lo_ref, jnp.greater, lo_thr.astype(I16))
    thr = hi_thr[0:1, :] * (2 * HALF) + (lo_thr[0:1, :] + HALF)
    need = jnp.where(thr == INT_MIN, 0.0, top_k - n_gt)

    zeros_q = jnp.zeros((HEAD_DIM, wide), BF16)
    qt_groups = []
    for g in range(N_KV_HEADS):
        qg = jnp.concatenate([qt_ref[0, g * Q_PER_KV + r] for r in range(Q_PER_KV)], axis=1)
        qt_groups.append(jnp.concatenate([qg, zeros_q] if g == 0 else [zeros_q, qg], axis=0))
    half = KEY_CHUNK // 2
    tri_r = lax.broadcasted_iota(I32, (KEY_CHUNK, KEY_CHUNK), 0)
    tri_c = lax.broadcasted_iota(I32, (KEY_CHUNK, KEY_CHUNK), 1)
    prefix_mat = jnp.where(tri_c <= tri_r, 1.0, 0.0).astype(BF16)

    m_ref[...] = jnp.full(m_ref.shape, M_INIT, F32)
    acc_ref[...] = jnp.zeros(acc_ref.shape, F32)
    piece = KEY_CHUNK // ATTN_PIECES

    def mask_bias(c, ties_before):
        off = pl.multiple_of(c * KEY_CHUNK, KEY_CHUNK)
        kc = key_ref[pl.ds(off, KEY_CHUNK), :]
        is_tie = kc == thr
        tie = jnp.where(is_tie, 1.0, 0.0).astype(BF16)
        rank_top = jnp.dot(prefix_mat[0:half, 0:half], tie[0:half], preferred_element_type=F32)
        rank_bot = jnp.dot(prefix_mat[half:KEY_CHUNK, :], tie, preferred_element_type=F32)
        tie_rank = ties_before + jnp.concatenate([rank_top, rank_bot], axis=0)
        admitted = jnp.where(is_tie, tie_rank, 2.0 * KEY_CHUNK * KEY_CHUNK) <= need
        bias = jnp.where(kc > thr, 0.0, jnp.where(admitted, 0.0, MASK_BIAS))
        return bias, tie_rank[KEY_CHUNK - 1:KEY_CHUNK, :]

    def raw_scores(c, g):
        off = pl.multiple_of(c * KEY_CHUNK, KEY_CHUNK)
        s_ref[g] = jnp.dot(k_ref[0, pl.ds(off, KEY_CHUNK), :], qt_groups[g], preferred_element_type=F32)

    def softmax_pv(c, g, bias):
        off = pl.multiple_of(c * KEY_CHUNK, KEY_CHUNK)
        mx = None
        for j in range(ATTN_PIECES):
            rows = slice(j * piece, (j + 1) * piece)
            s = s_ref[g, rows, :] + jnp.concatenate([bias[rows]] * Q_PER_KV, axis=1)
            s_ref[g, rows, :] = s
            pm = jnp.max(s.reshape(piece // SUBLANES, SUBLANES, wide), axis=0)
            mx = pm if mx is None else jnp.maximum(mx, pm)
        m_old = m_ref[g]
        m_new = jnp.maximum(m_old, jnp.max(mx, axis=0, keepdims=True))
        m_ref[g] = m_new
        for j in range(ATTN_PIECES):
            rows = slice(j * piece, (j + 1) * piece)
            p_ref[g, rows, :] = jnp.exp2(s_ref[g, rows, :] - m_new).astype(BF16)
        pv = jnp.dot(vta_ref[0, g, :, pl.ds(off, KEY_CHUNK)], p_ref[g], preferred_element_type=F32)
        acc_ref[g] = jnp.exp2(m_old - m_new) * acc_ref[g] + pv

    bias0, ties0 = mask_bias(0, jnp.zeros((1, Q_BLOCK), F32))
    bias_ref[...] = bias0
    raw_scores(0, 0)

    def attn_body(c, ties_before):
        nxt = jnp.minimum(c + 1, n_chunks - 1)
        bias = bias_ref[...]
        raw_scores(c, 1)
        softmax_pv(c, 0, bias)
        bias_next, ties_next = mask_bias(nxt, ties_before)
        raw_scores(nxt, 0)
        softmax_pv(c, 1, bias)
        bias_ref[...] = bias_next
        return ties_next

    lax.fori_loop(0, n_chunks, attn_body, ties0)

    for g in range(N_KV_HEADS):
        acc = acc_ref[g]
        out = acc[0:HEAD_DIM, :] / acc[HEAD_DIM:HEAD_DIM + 1, :]
        for r in range(0, Q_PER_KV, 2):
            pair = jnp.concatenate([out[:, r * Q_BLOCK:(r + 1) * Q_BLOCK], out[:, (r + 1) * Q_BLOCK:(r + 2) * Q_BLOCK]],
                                   axis=0)
            col = (g * Q_PER_KV + r) * HEAD_DIM
            o_ref[0, :, col:col + 2 * HEAD_DIM] = pair.T.astype(o_ref.dtype)


def _dsa(qt, qit, wt, kib, k, vta, top_k):
    b, _, _, l = qt.shape
    assert l // PACK <= 256, "bf16 counters hold at most 256 keys per slot"
    wide = Q_PER_KV * Q_BLOCK
    return pl.pallas_call(
        functools.partial(_dsa_kernel, top_k),
        grid=(b, l // Q_BLOCK),
        in_specs=[
            pl.BlockSpec((1, N_Q_HEADS, HEAD_DIM, Q_BLOCK), lambda bi, i: (bi, 0, 0, i)),
            pl.BlockSpec((1, IDX_HEADS, IDX_DIM, Q_BLOCK), lambda bi, i: (bi, 0, 0, i)),
            pl.BlockSpec((1, SUBLANES, Q_BLOCK), lambda bi, i: (bi, 0, i)),
            pl.BlockSpec((1, l, LANES), lambda bi, i: (bi, 0, 0)),
            pl.BlockSpec((1, l, LANES), lambda bi, i: (bi, 0, 0)),
            pl.BlockSpec((1, N_KV_HEADS, V_ROWS, l), lambda bi, i: (bi, 0, 0, 0)),
        ],
        out_specs=pl.BlockSpec((1, Q_BLOCK, ATTN_WIDTH), lambda bi, i: (bi, i, 0)),
        out_shape=jax.ShapeDtypeStruct((b, l, ATTN_WIDTH), BF16),
        scratch_shapes=[
            pltpu.VMEM((l, Q_BLOCK), I32),
            pltpu.VMEM((l // PACK, PACK, Q_BLOCK), I16),
            pltpu.VMEM((l // PACK, PACK, Q_BLOCK), I16),
            pltpu.VMEM((KEY_CHUNK, Q_BLOCK), F32),
            pltpu.VMEM((N_KV_HEADS, KEY_CHUNK, wide), F32),
            pltpu.VMEM((N_KV_HEADS, KEY_CHUNK, wide), BF16),
            pltpu.VMEM((N_KV_HEADS, 1, wide), F32),
            pltpu.VMEM((N_KV_HEADS, V_ROWS, wide), F32),
        ],
        compiler_params=pltpu.CompilerParams(dimension_semantics=("arbitrary", "arbitrary"),
                                             vmem_limit_bytes=VMEM_LIMIT_BYTES),
        name="dsa",
    )(qt, qit, wt, kib, k, vta)


def _rglru_kernel(rx_ref, rgate_ref, cw_ref, cb_ref, wg_ref, ba_ref, bx_ref, lam_ref, o_ref, xbuf_ref, h_ref):
    rows = rx_ref.shape[1]
    halo = SUBLANES

    @pl.when(pl.program_id(1) == 0)
    def _():
        xbuf_ref[0:halo, :] = jnp.zeros((halo, RNN_WIDTH), F32)
        h_ref[...] = jnp.zeros(h_ref.shape, F32)

    x = rx_ref[0].astype(F32)
    xbuf_ref[halo:halo + rows, :] = x
    rowi = lax.broadcasted_iota(I32, (rows, LANES), 0)

    for n in range(RNN_WIDTH // LANES):
        sl = slice(n * LANES, (n + 1) * LANES)
        xc = jnp.broadcast_to(cb_ref[:, sl], (rows, LANES))
        for j in range(CONV_WIDTH):
            start = halo - (CONV_WIDTH - 1) + j
            xc = xc + cw_ref[j:j + 1, sl] * xbuf_ref[start:start + rows, sl]
        gates = jnp.dot(xc.astype(BF16), wg_ref[n], preferred_element_type=F32)
        r = _sigmoid(gates[:, 0:LANES] + ba_ref[:, sl])
        i = _sigmoid(gates[:, LANES:2 * LANES] + bx_ref[:, sl])
        lam = lam_ref[:, sl]
        log_sig = jnp.minimum(lam, 0.0) - jnp.log1p(jnp.exp(-jnp.abs(lam)))
        log_a = LRU_C * r * log_sig
        a = jnp.exp(log_a)
        b = jnp.sqrt(1.0 - a * a) * (i * xc)
        d = 1
        while d < rows:
            a_prev = pltpu.roll(a, d, axis=0)
            b_prev = pltpu.roll(b, d, axis=0)
            live = rowi >= d
            b = jnp.where(live, a * b_prev + b, b)
            a = jnp.where(live, a * a_prev, a)
            d *= 2
        h = a * h_ref[:, sl] + b
        h_ref[:, sl] = h[rows - 1:rows, :]
        o_ref[0, :, sl] = (h * _gelu_tanh(rgate_ref[0, :, sl].astype(F32))).astype(o_ref.dtype)

    xbuf_ref[0:halo, :] = x[rows - halo:rows, :]


def _rglru(rx, rgate, cw, cb, wg, ba, bx, lam):
    b, l, _ = rx.shape
    rows = RNN_ROWS
    blk = lambda bi, i: (bi, i, 0)
    c2 = lambda bi, i: (0, 0)
    c3 = lambda bi, i: (0, 0, 0)
    return pl.pallas_call(
        _rglru_kernel,
        grid=(b, l // rows),
        in_specs=[
            pl.BlockSpec((1, rows, RNN_WIDTH), blk),
            pl.BlockSpec((1, rows, RNN_WIDTH), blk),
            pl.BlockSpec((CONV_WIDTH, RNN_WIDTH), c2),
            pl.BlockSpec((1, RNN_WIDTH), c2),
            pl.BlockSpec((RNN_WIDTH // LANES, LANES, 2 * LANES), c3),
            pl.BlockSpec((1, RNN_WIDTH), c2),
            pl.BlockSpec((1, RNN_WIDTH), c2),
            pl.BlockSpec((1, RNN_WIDTH), c2),
        ],
        out_specs=pl.BlockSpec((1, rows, RNN_WIDTH), blk),
        out_shape=jax.ShapeDtypeStruct((b, l, RNN_WIDTH), BF16),
        scratch_shapes=[
            pltpu.VMEM((rows + SUBLANES, RNN_WIDTH), F32),
            pltpu.VMEM((1, RNN_WIDTH), F32),
        ],
        compiler_params=pltpu.CompilerParams(dimension_semantics=("arbitrary", "arbitrary"),
                                             vmem_limit_bytes=VMEM_LIMIT_BYTES),
        name="rglru",
    )(rx, rgate, cw, cb, wg, ba, bx, lam)


def _merge_kernel(attn_ref, rnn_ref, ga_ref, gb_ref, x_ref, woa_ref, wor_ref, wout_ref, g2_ref, x1_ref, xn_ref):
    a = jnp.dot(attn_ref[...], woa_ref[...], preferred_element_type=F32)
    r = jnp.dot(rnn_ref[...], wor_ref[...], preferred_element_type=F32)
    merged = _sigmoid(ga_ref[...].astype(F32)) * a + _sigmoid(gb_ref[...].astype(F32)) * r
    x1 = x_ref[...] + jnp.dot(merged.astype(BF16), wout_ref[...], preferred_element_type=F32)
    x1_ref[...] = x1
    ms = jnp.mean(x1 * x1, axis=-1, keepdims=True)
    xn_ref[...] = (x1 * lax.rsqrt(ms + EPS) * g2_ref[...]).astype(BF16)


def _merge(attn, rnn, ga, gb, x2, woa, wor, wout, g2):
    m = x2.shape[0]
    rows = MERGE_ROWS
    row = lambda i: (i, 0)
    const = lambda i: (0, 0)
    return pl.pallas_call(
        _merge_kernel,
        grid=(m // rows,),
        in_specs=[
            pl.BlockSpec((rows, ATTN_WIDTH), row),
            pl.BlockSpec((rows, RNN_WIDTH), row),
            pl.BlockSpec((rows, D_MODEL), row),
            pl.BlockSpec((rows, D_MODEL), row),
            pl.BlockSpec((rows, D_MODEL), row),
            pl.BlockSpec((ATTN_WIDTH, D_MODEL), const, pipeline_mode=pl.Buffered(1)),
            pl.BlockSpec((RNN_WIDTH, D_MODEL), const, pipeline_mode=pl.Buffered(1)),
            pl.BlockSpec((D_MODEL, D_MODEL), const, pipeline_mode=pl.Buffered(1)),
            pl.BlockSpec((1, D_MODEL), const),
        ],
        out_specs=[pl.BlockSpec((rows, D_MODEL), row), pl.BlockSpec((rows, D_MODEL), row)],
        out_shape=[jax.ShapeDtypeStruct((m, D_MODEL), F32), jax.ShapeDtypeStruct((m, D_MODEL), BF16)],
        compiler_params=pltpu.CompilerParams(dimension_semantics=("arbitrary",), vmem_limit_bytes=VMEM_LIMIT_BYTES),
        name="merge",
    )(attn, rnn, ga, gb, x2, woa, wor, wout, g2)


def _ffn_kernel(xn_ref, x1_ref, wup_ref, cw_ref, cb_ref, wdn_ref, o_ref, tail_ref, acc_ref):
    rows = xn_ref.shape[1]

    @pl.when(pl.program_id(1) == 0)
    def _():
        tail_ref[...] = jnp.zeros(tail_ref.shape, F32)

    xn = xn_ref[0]
    rowi = lax.broadcasted_iota(I32, (rows, FFN_CHUNK), 0)
    first = rowi == 0
    second = rowi == 1

    def conv(col0):
        u = jnp.dot(xn, wup_ref[:, col0:col0 + FFN_CHUNK], preferred_element_type=F32)
        prev1 = tail_ref[SUBLANES - 1:SUBLANES, col0:col0 + FFN_CHUNK]
        prev2 = tail_ref[SUBLANES - 2:SUBLANES - 1, col0:col0 + FFN_CHUNK]
        tail_ref[:, col0:col0 + FFN_CHUNK] = u[rows - SUBLANES:rows, :]
        u1 = jnp.where(first, prev1, pltpu.roll(u, 1, axis=0))
        u2 = jnp.where(first, prev2, jnp.where(second, prev1, pltpu.roll(u, 2, axis=0)))
        cw = cw_ref[:, col0:col0 + FFN_CHUNK]
        return cb_ref[:, col0:col0 + FFN_CHUNK] + cw[0:1] * u2 + cw[1:2] * u1 + cw[2:3] * u

    for j in range(D_FF // FFN_CHUNK):
        gate = conv(j * FFN_CHUNK)
        val = conv(D_FF + j * FFN_CHUNK)
        hid = (gate * _sigmoid(gate) * val).astype(BF16)
        part = jnp.dot(hid, wdn_ref[j * FFN_CHUNK:(j + 1) * FFN_CHUNK, :], preferred_element_type=F32)
        if j == 0:
            acc_ref[...] = part
        else:
            acc_ref[...] += part
    o_ref[0] = x1_ref[0] + acc_ref[...]


def _ffn(xn, x1, wup, cw, cb, wdn):
    b, l, _ = xn.shape
    rows = FFN_ROWS
    blk = lambda bi, i: (bi, i, 0)
    const = lambda bi, i: (0, 0)
    return pl.pallas_call(
        _ffn_kernel,
        grid=(b, l // rows),
        in_specs=[
            pl.BlockSpec((1, rows, D_MODEL), blk),
            pl.BlockSpec((1, rows, D_MODEL), blk),
            pl.BlockSpec((D_MODEL, 2 * D_FF), const, pipeline_mode=pl.Buffered(1)),
            pl.BlockSpec((FFN_CONV_WIDTH, 2 * D_FF), const),
            pl.BlockSpec((1, 2 * D_FF), const),
            pl.BlockSpec((D_FF, D_MODEL), const, pipeline_mode=pl.Buffered(1)),
        ],
        out_specs=pl.BlockSpec((1, rows, D_MODEL), blk),
        out_shape=jax.ShapeDtypeStruct((b, l, D_MODEL), F32),
        scratch_shapes=[
            pltpu.VMEM((SUBLANES, 2 * D_FF), F32),
            pltpu.VMEM((rows, D_MODEL), F32),
        ],
        compiler_params=pltpu.CompilerParams(dimension_semantics=("arbitrary", "arbitrary"),
                                             vmem_limit_bytes=VMEM_LIMIT_BYTES),
        name="ffn",
    )(xn, x1, wup, cw, cb, wdn)


def _block_diag_pairs(w):
    z = jnp.zeros((RNN_BLOCK_DIM, RNN_BLOCK_DIM), w.dtype)
    w = w.reshape(RNN_BLOCKS // 2, 2, RNN_BLOCK_DIM, RNN_BLOCK_DIM)
    top = jnp.concatenate([w[:, 0], jnp.broadcast_to(z, w[:, 0].shape)], axis=-1)
    bot = jnp.concatenate([jnp.broadcast_to(z, w[:, 1].shape), w[:, 1]], axis=-1)
    return jnp.concatenate([top, bot], axis=-2)


def _layer(x, top_k, norm1_g, w_in, q_norm_g, k_norm_g, kidx_norm_g, conv_w, conv_b, rg_wa, rg_ba, rg_wx, rg_bx,
           rg_lambda, w_o_attn, w_o_rnn, w_out, norm2_g, w_up, ffn_conv_w, ffn_conv_b, w_down):
    b, l, _ = x.shape
    m = b * l
    x2 = x.reshape(m, D_MODEL)

    n_head = ATTN_WIDTH + 2 * KV_WIDTH + IDX_HEADS * IDX_DIM + IDX_DIM + IDX_HEADS
    pad = jnp.zeros((D_MODEL, ATTN_SLAB - n_head), w_in.dtype)
    w_all = jnp.concatenate([w_in[:, :n_head], pad, w_in[:, n_head:]], axis=1).astype(BF16)
    pair = lambda g: jnp.tile(g, 2).reshape(1, LANES)
    kig = jnp.concatenate([kidx_norm_g, jnp.zeros((LANES - IDX_DIM,), F32)]).reshape(1, LANES)

    qt, qit, wt, kib, k, vta, rx, rgate, ga, gb = _proj(x2, norm1_g.reshape(1, D_MODEL), w_all,
                                                          pair(q_norm_g), pair(k_norm_g), kig, b, l)
    attn = _dsa(qt, qit, wt, kib, k, vta, top_k)

    wg = jnp.concatenate([_block_diag_pairs(rg_wa), _block_diag_pairs(rg_wx)], axis=-1).astype(BF16)
    rnn = _rglru(rx.reshape(b, l, RNN_WIDTH), rgate.reshape(b, l, RNN_WIDTH), conv_w, conv_b.reshape(1, -1), wg,
                 rg_ba.reshape(1, -1), rg_bx.reshape(1, -1), rg_lambda.reshape(1, -1))

    x1, xn2 = _merge(attn.reshape(m, ATTN_WIDTH), rnn.reshape(m, RNN_WIDTH), ga, gb, x2,
                     w_o_attn.astype(BF16), w_o_rnn.astype(BF16), w_out.astype(BF16), norm2_g.reshape(1, D_MODEL))

    out = _ffn(xn2.reshape(b, l, D_MODEL), x1.reshape(b, l, D_MODEL), w_up.astype(BF16), ffn_conv_w,
               ffn_conv_b.reshape(1, -1), w_down.astype(BF16))
    return out


def kernel(x, norm1_g, w_in, q_norm_g, k_norm_g, kidx_norm_g, conv_w, conv_b, rg_wa, rg_ba, rg_wx, rg_bx, rg_lambda,
           w_o_attn, w_o_rnn, w_out, norm2_g, w_up, ffn_conv_w, ffn_conv_b, w_down):
    l = x.shape[1]
    top_k = min(TOPK_MAX, l // 4)
    h = x
    for d in range(norm1_g.shape[0]):
        h = _layer(h, top_k, norm1_g[d], w_in[d], q_norm_g[d], k_norm_g[d], kidx_norm_g[d], conv_w[d], conv_b[d],
                   rg_wa[d], rg_ba[d], rg_wx[d], rg_bx[d], rg_lambda[d], w_o_attn[d], w_o_rnn[d], w_out[d],
                   norm2_g[d], w_up[d], ffn_conv_w[d], ffn_conv_b[d], w_down[d])
    return h
```

```python
import functools

import jax
import jax.numpy as jnp
import numpy as np
from jax import lax
from jax.experimental import pallas as pl
from jax.experimental.pallas import tpu as pltpu

F32 = jnp.float32
BF16 = jnp.bfloat16
I32 = jnp.int32
I16 = jnp.int16

D_MODEL = 1024
HEAD_DIM = 64
N_Q_HEADS = 8
N_KV_HEADS = 2
Q_PER_KV = N_Q_HEADS // N_KV_HEADS
ATTN_WIDTH = N_Q_HEADS * HEAD_DIM
KV_WIDTH = N_KV_HEADS * HEAD_DIM
IDX_HEADS = 4
IDX_DIM = 64
TOPK_MAX = 256
RNN_WIDTH = D_MODEL
RNN_BLOCKS = 16
RNN_BLOCK_DIM = RNN_WIDTH // RNN_BLOCKS
CONV_WIDTH = 4
LRU_C = 8.0
D_FF = 2816
FFN_CONV_WIDTH = 3
EPS = 1e-6

LANES = 128
SUBLANES = 8
PACK = 16
VMEM_LIMIT_BYTES = 56 * 1024 * 1024

PROJ_ROWS = 512
ATTN_SLAB = ATTN_WIDTH + 2 * KV_WIDTH + IDX_HEADS * IDX_DIM + LANES
Q_BLOCK = 128
KEY_CHUNK = 512
RNN_ROWS = 256
MERGE_ROWS = 512
FFN_ROWS = 512
FFN_CHUNK = 256

ATTN_PIECES = 4
ONES_ROWS = 16
V_ROWS = HEAD_DIM + ONES_ROWS

INT_MIN = -2 ** 31
HALF = 2 ** 15
LOG2E = 1.4426950408889634
MASK_BIAS = -1e30
M_INIT = -3e38


def _sigmoid(x):
    return 1.0 / (1.0 + jnp.exp(-x))


def _gelu_tanh(x):
    c = np.sqrt(2.0 / np.pi).astype(np.float32)
    return x * (0.5 * (1.0 + jnp.tanh(c * (x + 0.044715 * (x * x * x)))))


def _pair_norm(slab, gain):
    lane = lax.broadcasted_iota(I32, slab.shape, 1)
    lo = lane < HEAD_DIM
    sq = slab * slab
    s_lo = jnp.sum(jnp.where(lo, sq, 0.0), axis=-1, keepdims=True)
    s_hi = jnp.sum(jnp.where(lo, 0.0, sq), axis=-1, keepdims=True)
    inv = jnp.where(lo, lax.rsqrt(s_lo * (1.0 / HEAD_DIM) + EPS), lax.rsqrt(s_hi * (1.0 / HEAD_DIM) + EPS))
    return slab * inv * gain


def _proj_kernel(x_ref, g1_ref, w_ref, qg_ref, kg_ref, kig_ref,
                 qt_ref, qit_ref, wt_ref, kib_ref, k_ref, vta_ref, rx_ref, rgate_ref, ga_ref, gb_ref):
    x = x_ref[...]
    rows = x.shape[0]
    ms = jnp.mean(x * x, axis=-1, keepdims=True)
    xn = (x * lax.rsqrt(ms + EPS) * g1_ref[...]).astype(BF16)

    pa = jnp.dot(xn, w_ref[:, 0:ATTN_SLAB], preferred_element_type=F32)
    qg = qg_ref[...]
    kg = kg_ref[...]

    def put_heads(ref, first_head, slab):
        t = slab.T
        ref[0, first_head] = t[0:HEAD_DIM].astype(ref.dtype)
        ref[0, first_head + 1] = t[HEAD_DIM:2 * HEAD_DIM].astype(ref.dtype)

    scale = (HEAD_DIM ** -0.5) * LOG2E
    for j in range(ATTN_WIDTH // LANES):
        sl = slice(j * LANES, (j + 1) * LANES)
        put_heads(qt_ref, 2 * j, _pair_norm(pa[:, sl], qg) * scale)
    off = ATTN_WIDTH
    k_ref[0] = _pair_norm(pa[:, off:off + KV_WIDTH], kg).astype(BF16)
    off += KV_WIDTH
    put_heads(vta_ref.at[:, :, 0:HEAD_DIM, :], 0, pa[:, off:off + KV_WIDTH])
    ones = jnp.ones((ONES_ROWS, rows), BF16)
    for g in range(N_KV_HEADS):
        vta_ref[0, g, HEAD_DIM:HEAD_DIM + ONES_ROWS, :] = ones
    off += KV_WIDTH
    for j in range(IDX_HEADS * IDX_DIM // LANES):
        put_heads(qit_ref, 2 * j, pa[:, off + j * LANES:off + (j + 1) * LANES] * (IDX_DIM ** -0.5))
    off += IDX_HEADS * IDX_DIM
    tail = pa[:, off:off + LANES]
    lane = lax.broadcasted_iota(I32, tail.shape, 1)
    is_ki = lane < IDX_DIM
    s_ki = jnp.sum(jnp.where(is_ki, tail * tail, 0.0), axis=-1, keepdims=True)
    ki_n = tail * lax.rsqrt(s_ki * (1.0 / IDX_DIM) + EPS) * kig_ref[...]
    kiwi = jnp.where(is_ki, ki_n, tail * (IDX_HEADS ** -0.5))
    kib_ref[0] = kiwi.astype(BF16)
    wt_ref[0] = kiwi.T[IDX_DIM:IDX_DIM + SUBLANES]

    off = ATTN_SLAB
    for ref in (rx_ref, rgate_ref, ga_ref, gb_ref):
        ref[...] = jnp.dot(xn, w_ref[:, off:off + D_MODEL], preferred_element_type=F32).astype(BF16)
        off += D_MODEL


def _proj(x2, g1, w_all, qg, kg, kig, b, l):
    m = x2.shape[0]
    rows = PROJ_ROWS
    tpb = l // rows
    const = lambda i: (0, 0)
    row = lambda i: (i, 0)
    n_all = w_all.shape[1]
    head_major = lambda i: (i // tpb, 0, 0, i % tpb)
    seq_major = lambda i: (i // tpb, i % tpb, 0)
    wide = [(D_MODEL, BF16)] * 4
    return pl.pallas_call(
        _proj_kernel,
        grid=(m // rows,),
        in_specs=[
            pl.BlockSpec((rows, D_MODEL), row),
            pl.BlockSpec((1, D_MODEL), const),
            pl.BlockSpec((D_MODEL, n_all), const, pipeline_mode=pl.Buffered(1)),
            pl.BlockSpec((1, LANES), const),
            pl.BlockSpec((1, LANES), const),
            pl.BlockSpec((1, LANES), const),
        ],
        out_specs=[
            pl.BlockSpec((1, N_Q_HEADS, HEAD_DIM, rows), head_major),
            pl.BlockSpec((1, IDX_HEADS, IDX_DIM, rows), head_major),
            pl.BlockSpec((1, SUBLANES, rows), lambda i: (i // tpb, 0, i % tpb)),
            pl.BlockSpec((1, rows, LANES), seq_major),
            pl.BlockSpec((1, rows, LANES), seq_major),
            pl.BlockSpec((1, N_KV_HEADS, V_ROWS, rows), head_major),
        ] + [pl.BlockSpec((rows, w), row) for w, _ in wide],
        out_shape=[
            jax.ShapeDtypeStruct((b, N_Q_HEADS, HEAD_DIM, l), BF16),
            jax.ShapeDtypeStruct((b, IDX_HEADS, IDX_DIM, l), BF16),
            jax.ShapeDtypeStruct((b, SUBLANES, l), F32),
            jax.ShapeDtypeStruct((b, l, LANES), BF16),
            jax.ShapeDtypeStruct((b, l, LANES), BF16),
            jax.ShapeDtypeStruct((b, N_KV_HEADS, V_ROWS, l), BF16),
        ] + [jax.ShapeDtypeStruct((m, w), dt) for w, dt in wide],
        compiler_params=pltpu.CompilerParams(dimension_semantics=("arbitrary",), vmem_limit_bytes=VMEM_LIMIT_BYTES),
        name="proj",
    )(x2, g1, w_all, qg, kg, kig)


def _dsa_kernel(top_k, n_chunks, qt_ref, qit_ref, wt_ref, kib_ref, k_ref, vta_ref, o_ref,
                key_ref, hi_ref, lo_ref, s_ref, p_ref, m_ref, acc_ref):
    blk = (n_chunks - 1) * (KEY_CHUNK // Q_BLOCK) + pl.program_id(1)
    packed = KEY_CHUNK // PACK
    wide = Q_PER_KV * Q_BLOCK
    chunk = lambda c: slice(c * KEY_CHUNK, (c + 1) * KEY_CHUNK)

    qit = jnp.concatenate([qit_ref[0, h] for h in range(IDX_HEADS)], axis=1)
    qit = jnp.concatenate([qit, jnp.zeros_like(qit)], axis=0)
    wt = wt_ref[0]

    for c in range(n_chunks):
        d = jnp.dot(kib_ref[0, chunk(c), :], qit, preferred_element_type=F32)
        s = jnp.zeros((KEY_CHUNK, Q_BLOCK), F32)
        for h in range(IDX_HEADS):
            s = s + wt[h:h + 1, :] * jnp.maximum(d[:, h * Q_BLOCK:(h + 1) * Q_BLOCK], 0.0)
        s = jnp.where(s == 0.0, 0.0, s)
        bits = lax.bitcast_convert_type(s, I32)
        key = bits ^ ((bits >> 31) & 0x7FFFFFFF)
        if c == n_chunks - 1:
            q_pos = blk * Q_BLOCK + lax.broadcasted_iota(I32, (KEY_CHUNK, Q_BLOCK), 1)
            key_pos = c * KEY_CHUNK + lax.broadcasted_iota(I32, (KEY_CHUNK, Q_BLOCK), 0)
            key = jnp.where(key_pos <= q_pos, key, INT_MIN)
        key_ref[chunk(c), :] = key
        hi_ref[c * packed:(c + 1) * packed] = (key >> 16).astype(I16).reshape(packed, PACK, Q_BLOCK)
        lo_ref[c * packed:(c + 1) * packed] = ((key & 0xFFFF) - HALF).astype(I16).reshape(packed, PACK, Q_BLOCK)

    one = jnp.ones((PACK, Q_BLOCK), BF16)
    zero = jnp.zeros((PACK, Q_BLOCK), BF16)
    n_acc = 4

    def count16(ref, compare, level):
        accs = [zero] * n_acc
        for i in range(n_chunks * packed):
            accs[i % n_acc] = accs[i % n_acc] + jnp.where(compare(ref[i], level), one, zero)
        tot = (accs[0].astype(F32) + accs[1].astype(F32)) + (accs[2].astype(F32) + accs[3].astype(F32))
        return jnp.sum(tot, axis=0, keepdims=True)

    def select16(ref, target):
        def bit_body(it, cand):
            trial = cand + jnp.left_shift(jnp.int32(1), 15 - it)
            enough = count16(ref, jnp.greater_equal, trial.astype(I16)) >= target
            return jnp.where(enough, trial, cand)
        return lax.fori_loop(0, 16, bit_body, jnp.full((PACK, Q_BLOCK), -HALF, I32))

    k_f = jnp.full((1, Q_BLOCK), float(top_k), F32)
    hi_thr = select16(hi_ref, k_f)
    hi_thr16 = hi_thr.astype(I16)
    n_above = count16(hi_ref, jnp.greater, hi_thr16)
    for i in range(n_chunks * packed):
        lo_ref[i] = jnp.where(hi_ref[i] == hi_thr16, lo_ref[i], jnp.int16(-HALF))
    lo_thr = select16(lo_ref, k_f - n_above)
    n_gt = n_above + count16(lo_ref, jnp.greater, lo_thr.astype(I16))
    thr = hi_thr[0:1, :] * (2 * HALF) + (lo_thr[0:1, :] + HALF)
    need = jnp.where(thr == INT_MIN, 0.0, top_k - n_gt)

    zeros_q = jnp.zeros((HEAD_DIM, wide), BF16)
    qt_groups = []
    for g in range(N_KV_HEADS):
        qg = jnp.concatenate([qt_ref[0, g * Q_PER_KV + r] for r in range(Q_PER_KV)], axis=1)
        qt_groups.append(jnp.concatenate([qg, zeros_q] if g == 0 else [zeros_q, qg], axis=0))
    half = KEY_CHUNK // 2
    tri_r = lax.broadcasted_iota(I32, (KEY_CHUNK, KEY_CHUNK), 0)
    tri_c = lax.broadcasted_iota(I32, (KEY_CHUNK, KEY_CHUNK), 1)
    prefix_mat = jnp.where(tri_c <= tri_r, 1.0, 0.0).astype(BF16)

    m_ref[...] = jnp.full(m_ref.shape, M_INIT, F32)
    acc_ref[...] = jnp.zeros(acc_ref.shape, F32)
    piece = KEY_CHUNK // ATTN_PIECES

    def mask_bias(c, ties_before):
        kc = key_ref[chunk(c), :]
        is_tie = kc == thr
        tie = jnp.where(is_tie, 1.0, 0.0).astype(BF16)
        rank_top = jnp.dot(prefix_mat[0:half, 0:half], tie[0:half], preferred_element_type=F32)
        rank_bot = jnp.dot(prefix_mat[half:KEY_CHUNK, :], tie, preferred_element_type=F32)
        tie_rank = ties_before + jnp.concatenate([rank_top, rank_bot], axis=0)
        admitted = jnp.where(is_tie, tie_rank, 2.0 * KEY_CHUNK * KEY_CHUNK) <= need
        bias = jnp.where(kc > thr, 0.0, jnp.where(admitted, 0.0, MASK_BIAS))
        return bias, tie_rank[KEY_CHUNK - 1:KEY_CHUNK, :]

    def raw_scores(c, g):
        s_ref[g] = jnp.dot(k_ref[0, chunk(c), :], qt_groups[g], preferred_element_type=F32)

    def softmax_pv(c, g, bias):
        mx = None
        for j in range(ATTN_PIECES):
            rows = slice(j * piece, (j + 1) * piece)
            s = s_ref[g, rows, :] + jnp.concatenate([bias[rows]] * Q_PER_KV, axis=1)
            s_ref[g, rows, :] = s
            pm = jnp.max(s.reshape(piece // SUBLANES, SUBLANES, wide), axis=0)
            mx = pm if mx is None else jnp.maximum(mx, pm)
        m_old = m_ref[g]
        m_new = jnp.maximum(m_old, jnp.max(mx, axis=0, keepdims=True))
        m_ref[g] = m_new
        for j in range(ATTN_PIECES):
            rows = slice(j * piece, (j + 1) * piece)
            p_ref[g, rows, :] = jnp.exp2(s_ref[g, rows, :] - m_new).astype(BF16)
        pv = jnp.dot(vta_ref[0, g, :, chunk(c)], p_ref[g], preferred_element_type=F32)
        acc_ref[g] = jnp.exp2(m_old - m_new) * acc_ref[g] + pv

    bias, ties = mask_bias(0, jnp.zeros((1, Q_BLOCK), F32))
    raw_scores(0, 0)
    for c in range(n_chunks):
        raw_scores(c, 1)
        softmax_pv(c, 0, bias)
        if c + 1 < n_chunks:
            bias_next, ties = mask_bias(c + 1, ties)
            raw_scores(c + 1, 0)
        softmax_pv(c, 1, bias)
        if c + 1 < n_chunks:
            bias = bias_next

    for g in range(N_KV_HEADS):
        acc = acc_ref[g]
        out = acc[0:HEAD_DIM, :] / acc[HEAD_DIM:HEAD_DIM + 1, :]
        for r in range(0, Q_PER_KV, 2):
            pair = jnp.concatenate([out[:, r * Q_BLOCK:(r + 1) * Q_BLOCK], out[:, (r + 1) * Q_BLOCK:(r + 2) * Q_BLOCK]],
                                   axis=0)
            col = (g * Q_PER_KV + r) * HEAD_DIM
            o_ref[0, :, col:col + 2 * HEAD_DIM] = pair.T.astype(o_ref.dtype)


def _dsa(qt, qit, wt, kib, k, vta, top_k):
    b, _, _, l = qt.shape
    assert l // PACK <= 256, "bf16 counters hold at most 256 keys per slot"
    wide = Q_PER_KV * Q_BLOCK
    per = KEY_CHUNK // Q_BLOCK
    outs = []
    for n_chunks in range(1, l // KEY_CHUNK + 1):
        first = (n_chunks - 1) * per
        keys = n_chunks * KEY_CHUNK
        outs.append(pl.pallas_call(
            functools.partial(_dsa_kernel, top_k, n_chunks),
            grid=(b, per),
            in_specs=[
                pl.BlockSpec((1, N_Q_HEADS, HEAD_DIM, Q_BLOCK), lambda bi, i, first=first: (bi, 0, 0, first + i)),
                pl.BlockSpec((1, IDX_HEADS, IDX_DIM, Q_BLOCK), lambda bi, i, first=first: (bi, 0, 0, first + i)),
                pl.BlockSpec((1, SUBLANES, Q_BLOCK), lambda bi, i, first=first: (bi, 0, first + i)),
                pl.BlockSpec((1, keys, LANES), lambda bi, i: (bi, 0, 0)),
                pl.BlockSpec((1, keys, LANES), lambda bi, i: (bi, 0, 0)),
                pl.BlockSpec((1, N_KV_HEADS, V_ROWS, keys), lambda bi, i: (bi, 0, 0, 0)),
            ],
            out_specs=pl.BlockSpec((1, Q_BLOCK, ATTN_WIDTH), lambda bi, i: (bi, i, 0)),
            out_shape=jax.ShapeDtypeStruct((b, KEY_CHUNK, ATTN_WIDTH), BF16),
            scratch_shapes=[
                pltpu.VMEM((keys, Q_BLOCK), I32),
                pltpu.VMEM((keys // PACK, PACK, Q_BLOCK), I16),
                pltpu.VMEM((keys // PACK, PACK, Q_BLOCK), I16),
                pltpu.VMEM((N_KV_HEADS, KEY_CHUNK, wide), F32),
                pltpu.VMEM((N_KV_HEADS, KEY_CHUNK, wide), BF16),
                pltpu.VMEM((N_KV_HEADS, 1, wide), F32),
                pltpu.VMEM((N_KV_HEADS, V_ROWS, wide), F32),
            ],
            compiler_params=pltpu.CompilerParams(dimension_semantics=("arbitrary", "arbitrary"),
                                                 vmem_limit_bytes=VMEM_LIMIT_BYTES),
            name=f"dsa{n_chunks}",
        )(qt, qit, wt, kib, k, vta))
    return jnp.concatenate(outs, axis=1)


def _rglru_kernel(rx_ref, rgate_ref, cw_ref, cb_ref, wg_ref, ba_ref, bx_ref, lam_ref, o_ref, xbuf_ref, h_ref):
    rows = rx_ref.shape[1]
    halo = SUBLANES

    @pl.when(pl.program_id(1) == 0)
    def _():
        xbuf_ref[0:halo, :] = jnp.zeros((halo, RNN_WIDTH), F32)
        h_ref[...] = jnp.zeros(h_ref.shape, F32)

    x = rx_ref[0].astype(F32)
    xbuf_ref[halo:halo + rows, :] = x
    groups = rows // SUBLANES
    sub = lax.broadcasted_iota(I32, (groups, SUBLANES, LANES), 1)

    for n in range(RNN_WIDTH // LANES):
        sl = slice(n * LANES, (n + 1) * LANES)
        xc = jnp.broadcast_to(cb_ref[:, sl], (rows, LANES))
        for j in range(CONV_WIDTH):
            start = halo - (CONV_WIDTH - 1) + j
            xc = xc + cw_ref[j:j + 1, sl] * xbuf_ref[start:start + rows, sl]
        gates = jnp.dot(xc.astype(BF16), wg_ref[n], preferred_element_type=F32)
        r = _sigmoid(gates[:, 0:LANES] + ba_ref[:, sl])
        i = _sigmoid(gates[:, LANES:2 * LANES] + bx_ref[:, sl])
        lam = lam_ref[:, sl]
        log_sig = jnp.minimum(lam, 0.0) - jnp.log1p(jnp.exp(-jnp.abs(lam)))
        log_a = LRU_C * r * log_sig
        a = jnp.exp(log_a)
        b = jnp.sqrt(1.0 - a * a) * (i * xc)
        a3 = a.reshape(groups, SUBLANES, LANES)
        b3 = b.reshape(groups, SUBLANES, LANES)
        d = 1
        while d < SUBLANES:
            live = sub >= d
            a_prev = jnp.where(live, pltpu.roll(a3, d, axis=1), 1.0)
            b_prev = jnp.where(live, pltpu.roll(b3, d, axis=1), 0.0)
            b3 = a3 * b_prev + b3
            a3 = a3 * a_prev
            d *= 2
        h_prev = h_ref[:, sl]
        hs = []
        for t in range(groups):
            h_t = a3[t] * h_prev + b3[t]
            hs.append(h_t)
            h_prev = h_t[SUBLANES - 1:SUBLANES, :]
        h_ref[:, sl] = h_prev
        h = jnp.concatenate(hs, axis=0)
        o_ref[0, :, sl] = (h * _gelu_tanh(rgate_ref[0, :, sl].astype(F32))).astype(o_ref.dtype)

    xbuf_ref[0:halo, :] = x[rows - halo:rows, :]


def _rglru(rx, rgate, cw, cb, wg, ba, bx, lam):
    b, l, _ = rx.shape
    rows = RNN_ROWS
    blk = lambda bi, i: (bi, i, 0)
    c2 = lambda bi, i: (0, 0)
    c3 = lambda bi, i: (0, 0, 0)
    return pl.pallas_call(
        _rglru_kernel,
        grid=(b, l // rows),
        in_specs=[
            pl.BlockSpec((1, rows, RNN_WIDTH), blk),
            pl.BlockSpec((1, rows, RNN_WIDTH), blk),
            pl.BlockSpec((CONV_WIDTH, RNN_WIDTH), c2),
            pl.BlockSpec((1, RNN_WIDTH), c2),
            pl.BlockSpec((RNN_WIDTH // LANES, LANES, 2 * LANES), c3),
            pl.BlockSpec((1, RNN_WIDTH), c2),
            pl.BlockSpec((1, RNN_WIDTH), c2),
            pl.BlockSpec((1, RNN_WIDTH), c2),
        ],
        out_specs=pl.BlockSpec((1, rows, RNN_WIDTH), blk),
        out_shape=jax.ShapeDtypeStruct((b, l, RNN_WIDTH), BF16),
        scratch_shapes=[
            pltpu.VMEM((rows + SUBLANES, RNN_WIDTH), F32),
            pltpu.VMEM((1, RNN_WIDTH), F32),
        ],
        compiler_params=pltpu.CompilerParams(dimension_semantics=("arbitrary", "arbitrary"),
                                             vmem_limit_bytes=VMEM_LIMIT_BYTES),
        name="rglru",
    )(rx, rgate, cw, cb, wg, ba, bx, lam)


def _merge_kernel(attn_ref, rnn_ref, ga_ref, gb_ref, x_ref, woa_ref, wor_ref, wout_ref, g2_ref, x1_ref, xn_ref):
    a = jnp.dot(attn_ref[...], woa_ref[...], preferred_element_type=F32)
    r = jnp.dot(rnn_ref[...], wor_ref[...], preferred_element_type=F32)
    merged = _sigmoid(ga_ref[...].astype(F32)) * a + _sigmoid(gb_ref[...].astype(F32)) * r
    x1 = x_ref[...] + jnp.dot(merged.astype(BF16), wout_ref[...], preferred_element_type=F32)
    x1_ref[...] = x1
    ms = jnp.mean(x1 * x1, axis=-1, keepdims=True)
    xn_ref[...] = (x1 * lax.rsqrt(ms + EPS) * g2_ref[...]).astype(BF16)


def _merge(attn, rnn, ga, gb, x2, woa, wor, wout, g2):
    m = x2.shape[0]
    rows = MERGE_ROWS
    row = lambda i: (i, 0)
    const = lambda i: (0, 0)
    return pl.pallas_call(
        _merge_kernel,
        grid=(m // rows,),
        in_specs=[
            pl.BlockSpec((rows, ATTN_WIDTH), row),
            pl.BlockSpec((rows, RNN_WIDTH), row),
            pl.BlockSpec((rows, D_MODEL), row),
            pl.BlockSpec((rows, D_MODEL), row),
            pl.BlockSpec((rows, D_MODEL), row),
            pl.BlockSpec((ATTN_WIDTH, D_MODEL), const, pipeline_mode=pl.Buffered(1)),
            pl.BlockSpec((RNN_WIDTH, D_MODEL), const, pipeline_mode=pl.Buffered(1)),
            pl.BlockSpec((D_MODEL, D_MODEL), const, pipeline_mode=pl.Buffered(1)),
            pl.BlockSpec((1, D_MODEL), const),
        ],
        out_specs=[pl.BlockSpec((rows, D_MODEL), row), pl.BlockSpec((rows, D_MODEL), row)],
        out_shape=[jax.ShapeDtypeStruct((m, D_MODEL), F32), jax.ShapeDtypeStruct((m, D_MODEL), BF16)],
        compiler_params=pltpu.CompilerParams(dimension_semantics=("arbitrary",), vmem_limit_bytes=VMEM_LIMIT_BYTES),
        name="merge",
    )(attn, rnn, ga, gb, x2, woa, wor, wout, g2)


def _ffn_kernel(xn_ref, x1_ref, wup_ref, cw_ref, cb_ref, wdn_ref, o_ref, tail_ref, hid_ref):
    rows = xn_ref.shape[1]

    @pl.when(pl.program_id(1) == 0)
    def _():
        tail_ref[...] = jnp.zeros(tail_ref.shape, F32)

    xn = xn_ref[0]
    sub = lax.broadcasted_iota(I32, (SUBLANES, FFN_CHUNK), 0)

    def conv(col0):
        cols = slice(col0, col0 + FFN_CHUNK)
        u = jnp.dot(xn, wup_ref[:, cols], preferred_element_type=F32)
        cw = cw_ref[:, cols]
        cb = cb_ref[:, cols]
        taps = lambda u2, u1, u0: cb + cw[0:1] * u2 + cw[1:2] * u1 + cw[2:3] * u0
        y = taps(pltpu.roll(u, 2, axis=0), pltpu.roll(u, 1, axis=0), u)
        tail = tail_ref[:, cols]
        head = u[0:SUBLANES]
        prev1 = tail[SUBLANES - 1:SUBLANES]
        prev2 = tail[SUBLANES - 2:SUBLANES - 1]
        h1 = jnp.where(sub == 0, prev1, pltpu.roll(head, 1, axis=0))
        h2 = jnp.where(sub == 0, prev2, jnp.where(sub == 1, prev1, pltpu.roll(head, 2, axis=0)))
        tail_ref[:, cols] = u[rows - SUBLANES:rows]
        return jnp.concatenate([taps(h2, h1, head), y[SUBLANES:]], axis=0)

    for j in range(D_FF // FFN_CHUNK):
        gate = conv(j * FFN_CHUNK)
        val = conv(D_FF + j * FFN_CHUNK)
        hid_ref[:, j * FFN_CHUNK:(j + 1) * FFN_CHUNK] = (gate * _sigmoid(gate) * val).astype(BF16)
    o_ref[0] = x1_ref[0] + jnp.dot(hid_ref[...], wdn_ref[...], preferred_element_type=F32)


def _ffn(xn, x1, wup, cw, cb, wdn):
    b, l, _ = xn.shape
    rows = FFN_ROWS
    blk = lambda bi, i: (bi, i, 0)
    const = lambda bi, i: (0, 0)
    return pl.pallas_call(
        _ffn_kernel,
        grid=(b, l // rows),
        in_specs=[
            pl.BlockSpec((1, rows, D_MODEL), blk),
            pl.BlockSpec((1, rows, D_MODEL), blk),
            pl.BlockSpec((D_MODEL, 2 * D_FF), const, pipeline_mode=pl.Buffered(1)),
            pl.BlockSpec((FFN_CONV_WIDTH, 2 * D_FF), const),
            pl.BlockSpec((1, 2 * D_FF), const),
            pl.BlockSpec((D_FF, D_MODEL), const, pipeline_mode=pl.Buffered(1)),
        ],
        out_specs=pl.BlockSpec((1, rows, D_MODEL), blk),
        out_shape=jax.ShapeDtypeStruct((b, l, D_MODEL), F32),
        scratch_shapes=[
            pltpu.VMEM((SUBLANES, 2 * D_FF), F32),
            pltpu.VMEM((rows, D_FF), BF16),
        ],
        compiler_params=pltpu.CompilerParams(dimension_semantics=("arbitrary", "arbitrary"),
                                             vmem_limit_bytes=VMEM_LIMIT_BYTES),
        name="ffn",
    )(xn, x1, wup, cw, cb, wdn)


def _block_diag_pairs(w):
    z = jnp.zeros((RNN_BLOCK_DIM, RNN_BLOCK_DIM), w.dtype)
    w = w.reshape(RNN_BLOCKS // 2, 2, RNN_BLOCK_DIM, RNN_BLOCK_DIM)
    top = jnp.concatenate([w[:, 0], jnp.broadcast_to(z, w[:, 0].shape)], axis=-1)
    bot = jnp.concatenate([jnp.broadcast_to(z, w[:, 1].shape), w[:, 1]], axis=-1)
    return jnp.concatenate([top, bot], axis=-2)


def _layer(x, top_k, norm1_g, w_in, q_norm_g, k_norm_g, kidx_norm_g, conv_w, conv_b, rg_wa, rg_ba, rg_wx, rg_bx,
           rg_lambda, w_o_attn, w_o_rnn, w_out, norm2_g, w_up, ffn_conv_w, ffn_conv_b, w_down):
    b, l, _ = x.shape
    m = b * l
    x2 = x.reshape(m, D_MODEL)

    n_head = ATTN_WIDTH + 2 * KV_WIDTH + IDX_HEADS * IDX_DIM + IDX_DIM + IDX_HEADS
    pad = jnp.zeros((D_MODEL, ATTN_SLAB - n_head), w_in.dtype)
    w_all = jnp.concatenate([w_in[:, :n_head], pad, w_in[:, n_head:]], axis=1).astype(BF16)
    pair = lambda g: jnp.tile(g, 2).reshape(1, LANES)
    kig = jnp.concatenate([kidx_norm_g, jnp.zeros((LANES - IDX_DIM,), F32)]).reshape(1, LANES)

    qt, qit, wt, kib, k, vta, rx, rgate, ga, gb = _proj(x2, norm1_g.reshape(1, D_MODEL), w_all,
                                                          pair(q_norm_g), pair(k_norm_g), kig, b, l)
    attn = _dsa(qt, qit, wt, kib, k, vta, top_k)

    wg = jnp.concatenate([_block_diag_pairs(rg_wa), _block_diag_pairs(rg_wx)], axis=-1).astype(BF16)
    rnn = _rglru(rx.reshape(b, l, RNN_WIDTH), rgate.reshape(b, l, RNN_WIDTH), conv_w, conv_b.reshape(1, -1), wg,
                 rg_ba.reshape(1, -1), rg_bx.reshape(1, -1), rg_lambda.reshape(1, -1))

    x1, xn2 = _merge(attn.reshape(m, ATTN_WIDTH), rnn.reshape(m, RNN_WIDTH), ga, gb, x2,
                     w_o_attn.astype(BF16), w_o_rnn.astype(BF16), w_out.astype(BF16), norm2_g.reshape(1, D_MODEL))

    out = _ffn(xn2.reshape(b, l, D_MODEL), x1.reshape(b, l, D_MODEL), w_up.astype(BF16), ffn_conv_w,
               ffn_conv_b.reshape(1, -1), w_down.astype(BF16))
    return out


def kernel(x, norm1_g, w_in, q_norm_g, k_norm_g, kidx_norm_g, conv_w, conv_b, rg_wa, rg_ba, rg_wx, rg_bx, rg_lambda,
           w_o_attn, w_o_rnn, w_out, norm2_g, w_up, ffn_conv_w, ffn_conv_b, w_down):
    l = x.shape[1]
    top_k = min(TOPK_MAX, l // 4)
    h = x
    for d in range(norm1_g.shape[0]):
        h = _layer(h, top_k, norm1_g[d], w_in[d], q_norm_g[d], k_norm_g[d], kidx_norm_g[d], conv_w[d], conv_b[d],
                   rg_wa[d], rg_ba[d], rg_wx[d], rg_bx[d], rg_lambda[d], w_o_attn[d], w_o_rnn[d], w_out[d],
                   norm2_g[d], w_up[d], ffn_conv_w[d], ffn_conv_b[d], w_down[d])
    return h
```

```python
import functools

import jax
import jax.numpy as jnp
import numpy as np
from jax import lax
from jax.experimental import pallas as pl
from jax.experimental.pallas import tpu as pltpu

F32 = jnp.float32
BF16 = jnp.bfloat16
I32 = jnp.int32

D_MODEL = 1024
HEAD_DIM = 64
N_Q_HEADS = 8
N_KV_HEADS = 2
Q_PER_KV = N_Q_HEADS // N_KV_HEADS
ATTN_WIDTH = N_Q_HEADS * HEAD_DIM
KV_WIDTH = N_KV_HEADS * HEAD_DIM
IDX_HEADS = 4
IDX_DIM = 64
TOPK_MAX = 256
RNN_WIDTH = D_MODEL
RNN_BLOCKS = 16
RNN_BLOCK_DIM = RNN_WIDTH // RNN_BLOCKS
CONV_WIDTH = 4
LRU_C = 8.0
D_FF = 2816
FFN_CONV_WIDTH = 3
EPS = 1e-6

LANES = 128
SUBLANES = 8
VMEM_LIMIT_BYTES = 56 * 1024 * 1024

PROJ_ROWS = 512
ATTN_SLAB = ATTN_WIDTH + 2 * KV_WIDTH + IDX_HEADS * IDX_DIM + LANES
Q_BLOCK = 256
KEY_CHUNK = 512
RNN_ROWS = 256
MERGE_ROWS = 512
FFN_ROWS = 512
FFN_CHUNK = 256

ATTN_PIECES = 4
ONES_ROWS = 16
V_ROWS = HEAD_DIM + ONES_ROWS

INT_MIN = -2 ** 31
LOG2E = 1.4426950408889634
MASK_BIAS = -1e30
M_INIT = -3e38


def _sigmoid(x):
    return 1.0 / (1.0 + jnp.exp(-x))


def _gelu_tanh(x):
    c = np.sqrt(2.0 / np.pi).astype(np.float32)
    return x * (0.5 * (1.0 + jnp.tanh(c * (x + 0.044715 * (x * x * x)))))


def _pair_norm(slab, gain):
    lane = lax.broadcasted_iota(I32, slab.shape, 1)
    lo = lane < HEAD_DIM
    sq = slab * slab
    s_lo = jnp.sum(jnp.where(lo, sq, 0.0), axis=-1, keepdims=True)
    s_hi = jnp.sum(jnp.where(lo, 0.0, sq), axis=-1, keepdims=True)
    inv = jnp.where(lo, lax.rsqrt(s_lo * (1.0 / HEAD_DIM) + EPS), lax.rsqrt(s_hi * (1.0 / HEAD_DIM) + EPS))
    return slab * inv * gain


def _proj_kernel(x_ref, g1_ref, w_ref, qg_ref, kg_ref, kig_ref,
                 qt_ref, qit_ref, wt_ref, kib_ref, k_ref, vta_ref, rx_ref, rgate_ref, ga_ref, gb_ref):
    x = x_ref[...]
    rows = x.shape[0]
    ms = jnp.mean(x * x, axis=-1, keepdims=True)
    xn = (x * lax.rsqrt(ms + EPS) * g1_ref[...]).astype(BF16)

    pa = jnp.dot(xn, w_ref[:, 0:ATTN_SLAB], preferred_element_type=F32)
    qg = qg_ref[...]
    kg = kg_ref[...]

    def put_heads(ref, first_head, slab):
        t = slab.T
        ref[0, first_head] = t[0:HEAD_DIM].astype(ref.dtype)
        ref[0, first_head + 1] = t[HEAD_DIM:2 * HEAD_DIM].astype(ref.dtype)

    scale = (HEAD_DIM ** -0.5) * LOG2E
    for j in range(ATTN_WIDTH // LANES):
        sl = slice(j * LANES, (j + 1) * LANES)
        put_heads(qt_ref, 2 * j, _pair_norm(pa[:, sl], qg) * scale)
    off = ATTN_WIDTH
    k_ref[0] = _pair_norm(pa[:, off:off + KV_WIDTH], kg).astype(BF16)
    off += KV_WIDTH
    put_heads(vta_ref.at[:, :, 0:HEAD_DIM, :], 0, pa[:, off:off + KV_WIDTH])
    ones = jnp.ones((ONES_ROWS, rows), BF16)
    for g in range(N_KV_HEADS):
        vta_ref[0, g, HEAD_DIM:HEAD_DIM + ONES_ROWS, :] = ones
    off += KV_WIDTH
    for j in range(IDX_HEADS * IDX_DIM // LANES):
        put_heads(qit_ref, 2 * j, pa[:, off + j * LANES:off + (j + 1) * LANES] * (IDX_DIM ** -0.5))
    off += IDX_HEADS * IDX_DIM
    tail = pa[:, off:off + LANES]
    lane = lax.broadcasted_iota(I32, tail.shape, 1)
    is_ki = lane < IDX_DIM
    s_ki = jnp.sum(jnp.where(is_ki, tail * tail, 0.0), axis=-1, keepdims=True)
    ki_n = tail * lax.rsqrt(s_ki * (1.0 / IDX_DIM) + EPS) * kig_ref[...]
    kiwi = jnp.where(is_ki, ki_n, tail * (IDX_HEADS ** -0.5))
    kib_ref[0] = kiwi.astype(BF16)
    wt_ref[0] = kiwi.T[IDX_DIM:IDX_DIM + SUBLANES]

    off = ATTN_SLAB
    for ref in (rx_ref, rgate_ref, ga_ref, gb_ref):
        ref[...] = jnp.dot(xn, w_ref[:, off:off + D_MODEL], preferred_element_type=F32).astype(BF16)
        off += D_MODEL


def _proj(x2, g1, w_all, qg, kg, kig, b, l):
    m = x2.shape[0]
    rows = PROJ_ROWS
    tpb = l // rows
    const = lambda i: (0, 0)
    row = lambda i: (i, 0)
    n_all = w_all.shape[1]
    head_major = lambda i: (i // tpb, 0, 0, i % tpb)
    seq_major = lambda i: (i // tpb, i % tpb, 0)
    wide = [(D_MODEL, BF16)] * 4
    return pl.pallas_call(
        _proj_kernel,
        grid=(m // rows,),
        in_specs=[
            pl.BlockSpec((rows, D_MODEL), row),
            pl.BlockSpec((1, D_MODEL), const),
            pl.BlockSpec((D_MODEL, n_all), const, pipeline_mode=pl.Buffered(1)),
            pl.BlockSpec((1, LANES), const),
            pl.BlockSpec((1, LANES), const),
            pl.BlockSpec((1, LANES), const),
        ],
        out_specs=[
            pl.BlockSpec((1, N_Q_HEADS, HEAD_DIM, rows), head_major),
            pl.BlockSpec((1, IDX_HEADS, IDX_DIM, rows), head_major),
            pl.BlockSpec((1, SUBLANES, rows), lambda i: (i // tpb, 0, i % tpb)),
            pl.BlockSpec((1, rows, LANES), seq_major),
            pl.BlockSpec((1, rows, LANES), seq_major),
            pl.BlockSpec((1, N_KV_HEADS, V_ROWS, rows), head_major),
        ] + [pl.BlockSpec((rows, w), row) for w, _ in wide],
        out_shape=[
            jax.ShapeDtypeStruct((b, N_Q_HEADS, HEAD_DIM, l), BF16),
            jax.ShapeDtypeStruct((b, IDX_HEADS, IDX_DIM, l), BF16),
            jax.ShapeDtypeStruct((b, SUBLANES, l), F32),
            jax.ShapeDtypeStruct((b, l, LANES), BF16),
            jax.ShapeDtypeStruct((b, l, LANES), BF16),
            jax.ShapeDtypeStruct((b, N_KV_HEADS, V_ROWS, l), BF16),
        ] + [jax.ShapeDtypeStruct((m, w), dt) for w, dt in wide],
        compiler_params=pltpu.CompilerParams(dimension_semantics=("arbitrary",), vmem_limit_bytes=VMEM_LIMIT_BYTES),
        name="proj",
    )(x2, g1, w_all, qg, kg, kig)


def _dsa_kernel(top_k, n_chunks, qt_ref, qit_ref, wt_ref, kib_ref, k_ref, vta_ref, o_ref,
                score_ref, s_ref, p_ref, m_ref, acc_ref):
    blk = (n_chunks - 1) * (KEY_CHUNK // Q_BLOCK) + pl.program_id(1)
    wide = Q_PER_KV * Q_BLOCK
    chunk = lambda c: slice(c * KEY_CHUNK, (c + 1) * KEY_CHUNK)

    qit = jnp.concatenate([qit_ref[0, h] for h in range(IDX_HEADS)], axis=1)
    qit = jnp.concatenate([qit, jnp.zeros_like(qit)], axis=0)
    wt = wt_ref[0]

    for c in range(n_chunks):
        d = jnp.dot(kib_ref[0, chunk(c), :], qit, preferred_element_type=F32)
        s = jnp.zeros((KEY_CHUNK, Q_BLOCK), F32)
        for h in range(IDX_HEADS):
            s = s + wt[h:h + 1, :] * jnp.maximum(d[:, h * Q_BLOCK:(h + 1) * Q_BLOCK], 0.0)
        if c == n_chunks - 1:
            q_pos = blk * Q_BLOCK + lax.broadcasted_iota(I32, (KEY_CHUNK, Q_BLOCK), 1)
            key_pos = c * KEY_CHUNK + lax.broadcasted_iota(I32, (KEY_CHUNK, Q_BLOCK), 0)
            s = jnp.where(key_pos <= q_pos, s, -jnp.inf)
        score_ref[chunk(c), :] = s

    n_acc = 4
    n_rows = n_chunks * KEY_CHUNK // SUBLANES

    def as_float(key):
        return lax.bitcast_convert_type(key ^ ((key >> 31) & 0x7FFFFFFF), F32)

    def count(compare, level):
        lvl = jnp.broadcast_to(level, (SUBLANES, Q_BLOCK))
        accs = [jnp.zeros((SUBLANES, Q_BLOCK), I32)] * n_acc
        for i in range(n_rows):
            hit = compare(score_ref[i * SUBLANES:(i + 1) * SUBLANES, :], lvl)
            accs[i % n_acc] = accs[i % n_acc] + jnp.where(hit, 1, 0)
        return jnp.sum((accs[0] + accs[1]) + (accs[2] + accs[3]), axis=0, keepdims=True)

    def bit_body(it, cand):
        trial = cand + jnp.left_shift(jnp.int32(1), 31 - it)
        return jnp.where(count(jnp.greater_equal, as_float(trial)) >= top_k, trial, cand)

    thr_key = lax.fori_loop(0, 32, bit_body, jnp.full((1, Q_BLOCK), INT_MIN, I32))
    thr = jnp.where(thr_key == INT_MIN, -jnp.inf, as_float(thr_key))
    need = jnp.where(thr == -jnp.inf, 0, top_k - count(jnp.greater, thr)).astype(F32)

    zeros_q = jnp.zeros((HEAD_DIM, wide), BF16)
    qt_groups = []
    for g in range(N_KV_HEADS):
        qg = jnp.concatenate([qt_ref[0, g * Q_PER_KV + r] for r in range(Q_PER_KV)], axis=1)
        qt_groups.append(jnp.concatenate([qg, zeros_q] if g == 0 else [zeros_q, qg], axis=0))
    half = KEY_CHUNK // 2
    tri_r = lax.broadcasted_iota(I32, (KEY_CHUNK, KEY_CHUNK), 0)
    tri_c = lax.broadcasted_iota(I32, (KEY_CHUNK, KEY_CHUNK), 1)
    prefix_mat = jnp.where(tri_c <= tri_r, 1.0, 0.0).astype(BF16)

    m_ref[...] = jnp.full(m_ref.shape, M_INIT, F32)
    acc_ref[...] = jnp.zeros(acc_ref.shape, F32)
    piece = KEY_CHUNK // ATTN_PIECES

    def mask_bias(c, ties_before):
        kc = score_ref[chunk(c), :]
        is_tie = kc == thr
        tie = jnp.where(is_tie, 1.0, 0.0).astype(BF16)
        rank_top = jnp.dot(prefix_mat[0:half, 0:half], tie[0:half], preferred_element_type=F32)
        rank_bot = jnp.dot(prefix_mat[half:KEY_CHUNK, :], tie, preferred_element_type=F32)
        tie_rank = ties_before + jnp.concatenate([rank_top, rank_bot], axis=0)
        admitted = jnp.where(is_tie, tie_rank, 2.0 * KEY_CHUNK * KEY_CHUNK) <= need
        bias = jnp.where(kc > thr, 0.0, jnp.where(admitted, 0.0, MASK_BIAS))
        return bias, tie_rank[KEY_CHUNK - 1:KEY_CHUNK, :]

    def raw_scores(c, g):
        s_ref[g] = jnp.dot(k_ref[0, chunk(c), :], qt_groups[g], preferred_element_type=F32)

    def softmax_pv(c, g, bias):
        mx = None
        for j in range(ATTN_PIECES):
            rows = slice(j * piece, (j + 1) * piece)
            s = s_ref[g, rows, :] + jnp.concatenate([bias[rows]] * Q_PER_KV, axis=1)
            s_ref[g, rows, :] = s
            pm = jnp.max(s.reshape(piece // SUBLANES, SUBLANES, wide), axis=0)
            mx = pm if mx is None else jnp.maximum(mx, pm)
        m_old = m_ref[g]
        m_new = jnp.maximum(m_old, jnp.max(mx, axis=0, keepdims=True))
        m_ref[g] = m_new
        for j in range(ATTN_PIECES):
            rows = slice(j * piece, (j + 1) * piece)
            p_ref[g, rows, :] = jnp.exp2(s_ref[g, rows, :] - m_new).astype(BF16)
        pv = jnp.dot(vta_ref[0, g, :, chunk(c)], p_ref[g], preferred_element_type=F32)
        acc_ref[g] = jnp.exp2(m_old - m_new) * acc_ref[g] + pv

    bias, ties = mask_bias(0, jnp.zeros((1, Q_BLOCK), F32))
    raw_scores(0, 0)
    for c in range(n_chunks):
        raw_scores(c, 1)
        softmax_pv(c, 0, bias)
        if c + 1 < n_chunks:
            bias_next, ties = mask_bias(c + 1, ties)
            raw_scores(c + 1, 0)
        softmax_pv(c, 1, bias)
        if c + 1 < n_chunks:
            bias = bias_next

    for g in range(N_KV_HEADS):
        acc = acc_ref[g]
        out = acc[0:HEAD_DIM, :] / acc[HEAD_DIM:HEAD_DIM + 1, :]
        for r in range(0, Q_PER_KV, 2):
            pair = jnp.concatenate([out[:, r * Q_BLOCK:(r + 1) * Q_BLOCK], out[:, (r + 1) * Q_BLOCK:(r + 2) * Q_BLOCK]],
                                   axis=0)
            col = (g * Q_PER_KV + r) * HEAD_DIM
            o_ref[0, :, col:col + 2 * HEAD_DIM] = pair.T.astype(o_ref.dtype)


def _dsa(qt, qit, wt, kib, k, vta, top_k):
    b, _, _, l = qt.shape
    wide = Q_PER_KV * Q_BLOCK
    per = KEY_CHUNK // Q_BLOCK
    outs = []
    for n_chunks in range(1, l // KEY_CHUNK + 1):
        first = (n_chunks - 1) * per
        keys = n_chunks * KEY_CHUNK
        outs.append(pl.pallas_call(
            functools.partial(_dsa_kernel, top_k, n_chunks),
            grid=(b, per),
            in_specs=[
                pl.BlockSpec((1, N_Q_HEADS, HEAD_DIM, Q_BLOCK), lambda bi, i, first=first: (bi, 0, 0, first + i)),
                pl.BlockSpec((1, IDX_HEADS, IDX_DIM, Q_BLOCK), lambda bi, i, first=first: (bi, 0, 0, first + i)),
                pl.BlockSpec((1, SUBLANES, Q_BLOCK), lambda bi, i, first=first: (bi, 0, first + i)),
                pl.BlockSpec((1, keys, LANES), lambda bi, i: (bi, 0, 0)),
                pl.BlockSpec((1, keys, LANES), lambda bi, i: (bi, 0, 0)),
                pl.BlockSpec((1, N_KV_HEADS, V_ROWS, keys), lambda bi, i: (bi, 0, 0, 0)),
            ],
            out_specs=pl.BlockSpec((1, Q_BLOCK, ATTN_WIDTH), lambda bi, i: (bi, i, 0)),
            out_shape=jax.ShapeDtypeStruct((b, KEY_CHUNK, ATTN_WIDTH), BF16),
            scratch_shapes=[
                pltpu.VMEM((keys, Q_BLOCK), F32),
                pltpu.VMEM((N_KV_HEADS, KEY_CHUNK, wide), F32),
                pltpu.VMEM((N_KV_HEADS, KEY_CHUNK, wide), BF16),
                pltpu.VMEM((N_KV_HEADS, 1, wide), F32),
                pltpu.VMEM((N_KV_HEADS, V_ROWS, wide), F32),
            ],
            compiler_params=pltpu.CompilerParams(dimension_semantics=("arbitrary", "arbitrary"),
                                                 vmem_limit_bytes=VMEM_LIMIT_BYTES),
            name=f"dsa{n_chunks}",
        )(qt, qit, wt, kib, k, vta))
    return jnp.concatenate(outs, axis=1)


def _rglru_kernel(rx_ref, rgate_ref, cw_ref, cb_ref, wg_ref, ba_ref, bx_ref, lam_ref, o_ref, xbuf_ref, h_ref):
    rows = rx_ref.shape[1]
    halo = SUBLANES

    @pl.when(pl.program_id(1) == 0)
    def _():
        xbuf_ref[0:halo, :] = jnp.zeros((halo, RNN_WIDTH), F32)
        h_ref[...] = jnp.zeros(h_ref.shape, F32)

    x = rx_ref[0].astype(F32)
    xbuf_ref[halo:halo + rows, :] = x
    groups = rows // SUBLANES
    sub = lax.broadcasted_iota(I32, (groups, SUBLANES, LANES), 1)

    for n in range(RNN_WIDTH // LANES):
        sl = slice(n * LANES, (n + 1) * LANES)
        xc = jnp.broadcast_to(cb_ref[:, sl], (rows, LANES))
        for j in range(CONV_WIDTH):
            start = halo - (CONV_WIDTH - 1) + j
            xc = xc + cw_ref[j:j + 1, sl] * xbuf_ref[start:start + rows, sl]
        gates = jnp.dot(xc.astype(BF16), wg_ref[n], preferred_element_type=F32)
        r = _sigmoid(gates[:, 0:LANES] + ba_ref[:, sl])
        i = _sigmoid(gates[:, LANES:2 * LANES] + bx_ref[:, sl])
        lam = lam_ref[:, sl]
        log_sig = jnp.minimum(lam, 0.0) - jnp.log1p(jnp.exp(-jnp.abs(lam)))
        log_a = LRU_C * r * log_sig
        a = jnp.exp(log_a)
        b = jnp.sqrt(1.0 - a * a) * (i * xc)
        a3 = a.reshape(groups, SUBLANES, LANES)
        b3 = b.reshape(groups, SUBLANES, LANES)
        d = 1
        while d < SUBLANES:
            live = sub >= d
            a_prev = jnp.where(live, pltpu.roll(a3, d, axis=1), 1.0)
            b_prev = jnp.where(live, pltpu.roll(b3, d, axis=1), 0.0)
            b3 = a3 * b_prev + b3
            a3 = a3 * a_prev
            d *= 2
        h_prev = h_ref[:, sl]
        hs = []
        for t in range(groups):
            h_t = a3[t] * h_prev + b3[t]
            hs.append(h_t)
            h_prev = h_t[SUBLANES - 1:SUBLANES, :]
        h_ref[:, sl] = h_prev
        h = jnp.concatenate(hs, axis=0)
        o_ref[0, :, sl] = (h * _gelu_tanh(rgate_ref[0, :, sl].astype(F32))).astype(o_ref.dtype)

    xbuf_ref[0:halo, :] = x[rows - halo:rows, :]


def _rglru(rx, rgate, cw, cb, wg, ba, bx, lam):
    b, l, _ = rx.shape
    rows = RNN_ROWS
    blk = lambda bi, i: (bi, i, 0)
    c2 = lambda bi, i: (0, 0)
    c3 = lambda bi, i: (0, 0, 0)
    return pl.pallas_call(
        _rglru_kernel,
        grid=(b, l // rows),
        in_specs=[
            pl.BlockSpec((1, rows, RNN_WIDTH), blk),
            pl.BlockSpec((1, rows, RNN_WIDTH), blk),
            pl.BlockSpec((CONV_WIDTH, RNN_WIDTH), c2),
            pl.BlockSpec((1, RNN_WIDTH), c2),
            pl.BlockSpec((RNN_WIDTH // LANES, LANES, 2 * LANES), c3),
            pl.BlockSpec((1, RNN_WIDTH), c2),
            pl.BlockSpec((1, RNN_WIDTH), c2),
            pl.BlockSpec((1, RNN_WIDTH), c2),
        ],
        out_specs=pl.BlockSpec((1, rows, RNN_WIDTH), blk),
        out_shape=jax.ShapeDtypeStruct((b, l, RNN_WIDTH), BF16),
        scratch_shapes=[
            pltpu.VMEM((rows + SUBLANES, RNN_WIDTH), F32),
            pltpu.VMEM((1, RNN_WIDTH), F32),
        ],
        compiler_params=pltpu.CompilerParams(dimension_semantics=("arbitrary", "arbitrary"),
                                             vmem_limit_bytes=VMEM_LIMIT_BYTES),
        name="rglru",
    )(rx, rgate, cw, cb, wg, ba, bx, lam)


def _merge_kernel(attn_ref, rnn_ref, ga_ref, gb_ref, x_ref, woa_ref, wor_ref, wout_ref, g2_ref, x1_ref, xn_ref):
    a = jnp.dot(attn_ref[...], woa_ref[...], preferred_element_type=F32)
    r = jnp.dot(rnn_ref[...], wor_ref[...], preferred_element_type=F32)
    merged = _sigmoid(ga_ref[...].astype(F32)) * a + _sigmoid(gb_ref[...].astype(F32)) * r
    x1 = x_ref[...] + jnp.dot(merged.astype(BF16), wout_ref[...], preferred_element_type=F32)
    x1_ref[...] = x1
    ms = jnp.mean(x1 * x1, axis=-1, keepdims=True)
    xn_ref[...] = (x1 * lax.rsqrt(ms + EPS) * g2_ref[...]).astype(BF16)


def _merge(attn, rnn, ga, gb, x2, woa, wor, wout, g2):
    m = x2.shape[0]
    rows = MERGE_ROWS
    row = lambda i: (i, 0)
    const = lambda i: (0, 0)
    return pl.pallas_call(
        _merge_kernel,
        grid=(m // rows,),
        in_specs=[
            pl.BlockSpec((rows, ATTN_WIDTH), row),
            pl.BlockSpec((rows, RNN_WIDTH), row),
            pl.BlockSpec((rows, D_MODEL), row),
            pl.BlockSpec((rows, D_MODEL), row),
            pl.BlockSpec((rows, D_MODEL), row),
            pl.BlockSpec((ATTN_WIDTH, D_MODEL), const, pipeline_mode=pl.Buffered(1)),
            pl.BlockSpec((RNN_WIDTH, D_MODEL), const, pipeline_mode=pl.Buffered(1)),
            pl.BlockSpec((D_MODEL, D_MODEL), const, pipeline_mode=pl.Buffered(1)),
            pl.BlockSpec((1, D_MODEL), const),
        ],
        out_specs=[pl.BlockSpec((rows, D_MODEL), row), pl.BlockSpec((rows, D_MODEL), row)],
        out_shape=[jax.ShapeDtypeStruct((m, D_MODEL), F32), jax.ShapeDtypeStruct((m, D_MODEL), BF16)],
        compiler_params=pltpu.CompilerParams(dimension_semantics=("arbitrary",), vmem_limit_bytes=VMEM_LIMIT_BYTES),
        name="merge",
    )(attn, rnn, ga, gb, x2, woa, wor, wout, g2)


def _ffn_kernel(xn_ref, x1_ref, wup_ref, cw_ref, cb_ref, wdn_ref, o_ref, tail_ref, hid_ref):
    rows = xn_ref.shape[1]

    @pl.when(pl.program_id(1) == 0)
    def _():
        tail_ref[...] = jnp.zeros(tail_ref.shape, F32)

    xn = xn_ref[0]
    sub = lax.broadcasted_iota(I32, (SUBLANES, FFN_CHUNK), 0)

    def conv(col0):
        cols = slice(col0, col0 + FFN_CHUNK)
        u = jnp.dot(xn, wup_ref[:, cols], preferred_element_type=F32)
        cw = cw_ref[:, cols]
        cb = cb_ref[:, cols]
        taps = lambda u2, u1, u0: cb + cw[0:1] * u2 + cw[1:2] * u1 + cw[2:3] * u0
        y = taps(pltpu.roll(u, 2, axis=0), pltpu.roll(u, 1, axis=0), u)
        tail = tail_ref[:, cols]
        head = u[0:SUBLANES]
        prev1 = tail[SUBLANES - 1:SUBLANES]
        prev2 = tail[SUBLANES - 2:SUBLANES - 1]
        h1 = jnp.where(sub == 0, prev1, pltpu.roll(head, 1, axis=0))
        h2 = jnp.where(sub == 0, prev2, jnp.where(sub == 1, prev1, pltpu.roll(head, 2, axis=0)))
        tail_ref[:, cols] = u[rows - SUBLANES:rows]
        return jnp.concatenate([taps(h2, h1, head), y[SUBLANES:]], axis=0)

    for j in range(D_FF // FFN_CHUNK):
        gate = conv(j * FFN_CHUNK)
        val = conv(D_FF + j * FFN_CHUNK)
        hid_ref[:, j * FFN_CHUNK:(j + 1) * FFN_CHUNK] = (gate * _sigmoid(gate) * val).astype(BF16)
    o_ref[0] = x1_ref[0] + jnp.dot(hid_ref[...], wdn_ref[...], preferred_element_type=F32)


def _ffn(xn, x1, wup, cw, cb, wdn):
    b, l, _ = xn.shape
    rows = FFN_ROWS
    blk = lambda bi, i: (bi, i, 0)
    const = lambda bi, i: (0, 0)
    return pl.pallas_call(
        _ffn_kernel,
        grid=(b, l // rows),
        in_specs=[
            pl.BlockSpec((1, rows, D_MODEL), blk),
            pl.BlockSpec((1, rows, D_MODEL), blk),
            pl.BlockSpec((D_MODEL, 2 * D_FF), const, pipeline_mode=pl.Buffered(1)),
            pl.BlockSpec((FFN_CONV_WIDTH, 2 * D_FF), const),
            pl.BlockSpec((1, 2 * D_FF), const),
            pl.BlockSpec((D_FF, D_MODEL), const, pipeline_mode=pl.Buffered(1)),
        ],
        out_specs=pl.BlockSpec((1, rows, D_MODEL), blk),
        out_shape=jax.ShapeDtypeStruct((b, l, D_MODEL), F32),
        scratch_shapes=[
            pltpu.VMEM((SUBLANES, 2 * D_FF), F32),
            pltpu.VMEM((rows, D_FF), BF16),
        ],
        compiler_params=pltpu.CompilerParams(dimension_semantics=("arbitrary", "arbitrary"),
                                             vmem_limit_bytes=VMEM_LIMIT_BYTES),
        name="ffn",
    )(xn, x1, wup, cw, cb, wdn)


def _block_diag_pairs(w):
    z = jnp.zeros((RNN_BLOCK_DIM, RNN_BLOCK_DIM), w.dtype)
    w = w.reshape(RNN_BLOCKS // 2, 2, RNN_BLOCK_DIM, RNN_BLOCK_DIM)
    top = jnp.concatenate([w[:, 0], jnp.broadcast_to(z, w[:, 0].shape)], axis=-1)
    bot = jnp.concatenate([jnp.broadcast_to(z, w[:, 1].shape), w[:, 1]], axis=-1)
    return jnp.concatenate([top, bot], axis=-2)


def _layer(x, top_k, norm1_g, w_in, q_norm_g, k_norm_g, kidx_norm_g, conv_w, conv_b, rg_wa, rg_ba, rg_wx, rg_bx,
           rg_lambda, w_o_attn, w_o_rnn, w_out, norm2_g, w_up, ffn_conv_w, ffn_conv_b, w_down):
    b, l, _ = x.shape
    m = b * l
    x2 = x.reshape(m, D_MODEL)

    n_head = ATTN_WIDTH + 2 * KV_WIDTH + IDX_HEADS * IDX_DIM + IDX_DIM + IDX_HEADS
    pad = jnp.zeros((D_MODEL, ATTN_SLAB - n_head), w_in.dtype)
    w_all = jnp.concatenate([w_in[:, :n_head], pad, w_in[:, n_head:]], axis=1).astype(BF16)
    pair = lambda g: jnp.tile(g, 2).reshape(1, LANES)
    kig = jnp.concatenate([kidx_norm_g, jnp.zeros((LANES - IDX_DIM,), F32)]).reshape(1, LANES)

    qt, qit, wt, kib, k, vta, rx, rgate, ga, gb = _proj(x2, norm1_g.reshape(1, D_MODEL), w_all,
                                                          pair(q_norm_g), pair(k_norm_g), kig, b, l)
    attn = _dsa(qt, qit, wt, kib, k, vta, top_k)

    wg = jnp.concatenate([_block_diag_pairs(rg_wa), _block_diag_pairs(rg_wx)], axis=-1).astype(BF16)
    rnn = _rglru(rx.reshape(b, l, RNN_WIDTH), rgate.reshape(b, l, RNN_WIDTH), conv_w, conv_b.reshape(1, -1), wg,
                 rg_ba.reshape(1, -1), rg_bx.reshape(1, -1), rg_lambda.reshape(1, -1))

    x1, xn2 = _merge(attn.reshape(m, ATTN_WIDTH), rnn.reshape(m, RNN_WIDTH), ga, gb, x2,
                     w_o_attn.astype(BF16), w_o_rnn.astype(BF16), w_out.astype(BF16), norm2_g.reshape(1, D_MODEL))

    out = _ffn(xn2.reshape(b, l, D_MODEL), x1.reshape(b, l, D_MODEL), w_up.astype(BF16), ffn_conv_w,
               ffn_conv_b.reshape(1, -1), w_down.astype(BF16))
    return out


def kernel(x, norm1_g, w_in, q_norm_g, k_norm_g, kidx_norm_g, conv_w, conv_b, rg_wa, rg_ba, rg_wx, rg_bx, rg_lambda,
           w_o_attn, w_o_rnn, w_out, norm2_g, w_up, ffn_conv_w, ffn_conv_b, w_down):
    l = x.shape[1]
    top_k = min(TOPK_MAX, l // 4)
    h = x
    for d in range(norm1_g.shape[0]):
        h = _layer(h, top_k, norm1_g[d], w_in[d], q_norm_g[d], k_norm_g[d], kidx_norm_g[d], conv_w[d], conv_b[d],
                   rg_wa[d], rg_ba[d], rg_wx[d], rg_bx[d], rg_lambda[d], w_o_attn[d], w_o_rnn[d], w_out[d],
                   norm2_g[d], w_up[d], ffn_conv_w[d], ffn_conv_b[d], w_down[d])
    return h
```

```python
import functools

import jax
import jax.numpy as jnp
import numpy as np
from jax import lax
from jax.experimental import pallas as pl
from jax.experimental.pallas import tpu as pltpu

F32 = jnp.float32
BF16 = jnp.bfloat16
I32 = jnp.int32

D_MODEL = 1024
HEAD_DIM = 64
N_Q_HEADS = 8
N_KV_HEADS = 2
Q_PER_KV = N_Q_HEADS // N_KV_HEADS
ATTN_WIDTH = N_Q_HEADS * HEAD_DIM
KV_WIDTH = N_KV_HEADS * HEAD_DIM
IDX_HEADS = 4
IDX_DIM = 64
TOPK_MAX = 256
RNN_WIDTH = D_MODEL
RNN_BLOCKS = 16
RNN_BLOCK_DIM = RNN_WIDTH // RNN_BLOCKS
CONV_WIDTH = 4
LRU_C = 8.0
D_FF = 2816
FFN_CONV_WIDTH = 3
EPS = 1e-6

LANES = 128
SUBLANES = 8
VMEM_LIMIT_BYTES = 56 * 1024 * 1024

PROJ_ROWS = 512
ATTN_SLAB = ATTN_WIDTH + 2 * KV_WIDTH + IDX_HEADS * IDX_DIM + LANES
Q_BLOCK = 512
KEY_CHUNK = 512
RNN_ROWS = 256
MERGE_ROWS = 512
FFN_ROWS = 512
FFN_CHUNK = 256

ATTN_PIECES = 4
ONES_ROWS = 16
V_ROWS = HEAD_DIM + ONES_ROWS

INT_MIN = -2 ** 31
LOG2E = 1.4426950408889634
MASK_BIAS = -1e30
M_INIT = -3e38


def _sigmoid(x):
    return 1.0 / (1.0 + jnp.exp(-x))


def _gelu_tanh(x):
    c = np.sqrt(2.0 / np.pi).astype(np.float32)
    return x * (0.5 * (1.0 + jnp.tanh(c * (x + 0.044715 * (x * x * x)))))


def _pair_norm(slab, gain):
    lane = lax.broadcasted_iota(I32, slab.shape, 1)
    lo = lane < HEAD_DIM
    sq = slab * slab
    s_lo = jnp.sum(jnp.where(lo, sq, 0.0), axis=-1, keepdims=True)
    s_hi = jnp.sum(jnp.where(lo, 0.0, sq), axis=-1, keepdims=True)
    inv = jnp.where(lo, lax.rsqrt(s_lo * (1.0 / HEAD_DIM) + EPS), lax.rsqrt(s_hi * (1.0 / HEAD_DIM) + EPS))
    return slab * inv * gain


def _proj_kernel(x_ref, g1_ref, w_ref, qg_ref, kg_ref, kig_ref,
                 qt_ref, qit_ref, wt_ref, kib_ref, k_ref, vta_ref, rx_ref, rgate_ref, ga_ref, gb_ref):
    x = x_ref[...]
    rows = x.shape[0]
    ms = jnp.mean(x * x, axis=-1, keepdims=True)
    xn = (x * lax.rsqrt(ms + EPS) * g1_ref[...]).astype(BF16)

    pa = jnp.dot(xn, w_ref[:, 0:ATTN_SLAB], preferred_element_type=F32)
    qg = qg_ref[...]
    kg = kg_ref[...]

    def put_heads(ref, first_head, slab):
        t = slab.T
        ref[0, first_head] = t[0:HEAD_DIM].astype(ref.dtype)
        ref[0, first_head + 1] = t[HEAD_DIM:2 * HEAD_DIM].astype(ref.dtype)

    scale = (HEAD_DIM ** -0.5) * LOG2E
    for j in range(ATTN_WIDTH // LANES):
        sl = slice(j * LANES, (j + 1) * LANES)
        put_heads(qt_ref, 2 * j, _pair_norm(pa[:, sl], qg) * scale)
    off = ATTN_WIDTH
    k_ref[0] = _pair_norm(pa[:, off:off + KV_WIDTH], kg).astype(BF16)
    off += KV_WIDTH
    put_heads(vta_ref.at[:, :, 0:HEAD_DIM, :], 0, pa[:, off:off + KV_WIDTH])
    ones = jnp.ones((ONES_ROWS, rows), BF16)
    for g in range(N_KV_HEADS):
        vta_ref[0, g, HEAD_DIM:HEAD_DIM + ONES_ROWS, :] = ones
    off += KV_WIDTH
    for j in range(IDX_HEADS * IDX_DIM // LANES):
        put_heads(qit_ref, 2 * j, pa[:, off + j * LANES:off + (j + 1) * LANES] * (IDX_DIM ** -0.5))
    off += IDX_HEADS * IDX_DIM
    tail = pa[:, off:off + LANES]
    lane = lax.broadcasted_iota(I32, tail.shape, 1)
    is_ki = lane < IDX_DIM
    s_ki = jnp.sum(jnp.where(is_ki, tail * tail, 0.0), axis=-1, keepdims=True)
    ki_n = tail * lax.rsqrt(s_ki * (1.0 / IDX_DIM) + EPS) * kig_ref[...]
    kiwi = jnp.where(is_ki, ki_n, tail * (IDX_HEADS ** -0.5))
    kib_ref[0] = kiwi.astype(BF16)
    wt_ref[0] = kiwi.T[IDX_DIM:IDX_DIM + SUBLANES]

    off = ATTN_SLAB
    for ref in (rx_ref, rgate_ref, ga_ref, gb_ref):
        ref[...] = jnp.dot(xn, w_ref[:, off:off + D_MODEL], preferred_element_type=F32).astype(BF16)
        off += D_MODEL


def _proj(x2, g1, w_all, qg, kg, kig, b, l):
    m = x2.shape[0]
    rows = PROJ_ROWS
    tpb = l // rows
    const = lambda i: (0, 0)
    row = lambda i: (i, 0)
    n_all = w_all.shape[1]
    head_major = lambda i: (i // tpb, 0, 0, i % tpb)
    seq_major = lambda i: (i // tpb, i % tpb, 0)
    wide = [(D_MODEL, BF16)] * 4
    return pl.pallas_call(
        _proj_kernel,
        grid=(m // rows,),
        in_specs=[
            pl.BlockSpec((rows, D_MODEL), row),
            pl.BlockSpec((1, D_MODEL), const),
            pl.BlockSpec((D_MODEL, n_all), const, pipeline_mode=pl.Buffered(1)),
            pl.BlockSpec((1, LANES), const),
            pl.BlockSpec((1, LANES), const),
            pl.BlockSpec((1, LANES), const),
        ],
        out_specs=[
            pl.BlockSpec((1, N_Q_HEADS, HEAD_DIM, rows), head_major),
            pl.BlockSpec((1, IDX_HEADS, IDX_DIM, rows), head_major),
            pl.BlockSpec((1, SUBLANES, rows), lambda i: (i // tpb, 0, i % tpb)),
            pl.BlockSpec((1, rows, LANES), seq_major),
            pl.BlockSpec((1, rows, LANES), seq_major),
            pl.BlockSpec((1, N_KV_HEADS, V_ROWS, rows), head_major),
        ] + [pl.BlockSpec((rows, w), row) for w, _ in wide],
        out_shape=[
            jax.ShapeDtypeStruct((b, N_Q_HEADS, HEAD_DIM, l), BF16),
            jax.ShapeDtypeStruct((b, IDX_HEADS, IDX_DIM, l), BF16),
            jax.ShapeDtypeStruct((b, SUBLANES, l), F32),
            jax.ShapeDtypeStruct((b, l, LANES), BF16),
            jax.ShapeDtypeStruct((b, l, LANES), BF16),
            jax.ShapeDtypeStruct((b, N_KV_HEADS, V_ROWS, l), BF16),
        ] + [jax.ShapeDtypeStruct((m, w), dt) for w, dt in wide],
        compiler_params=pltpu.CompilerParams(dimension_semantics=("arbitrary",), vmem_limit_bytes=VMEM_LIMIT_BYTES),
        name="proj",
    )(x2, g1, w_all, qg, kg, kig)


def _dsa_kernel(top_k, n_chunks, qt_ref, qit_ref, wt_ref, kib_ref, k_ref, vta_ref, o_ref,
                score_ref, s_ref, p_ref, m_ref, acc_ref):
    blk = (n_chunks - 1) * (KEY_CHUNK // Q_BLOCK) + pl.program_id(1)
    wide = Q_PER_KV * Q_BLOCK
    chunk = lambda c: slice(c * KEY_CHUNK, (c + 1) * KEY_CHUNK)

    qit = jnp.concatenate([qit_ref[0, h] for h in range(IDX_HEADS)], axis=1)
    qit = jnp.concatenate([qit, jnp.zeros_like(qit)], axis=0)
    wt = wt_ref[0]

    for c in range(n_chunks):
        d = jnp.dot(kib_ref[0, chunk(c), :], qit, preferred_element_type=F32)
        s = jnp.zeros((KEY_CHUNK, Q_BLOCK), F32)
        for h in range(IDX_HEADS):
            s = s + wt[h:h + 1, :] * jnp.maximum(d[:, h * Q_BLOCK:(h + 1) * Q_BLOCK], 0.0)
        if c == n_chunks - 1:
            q_pos = blk * Q_BLOCK + lax.broadcasted_iota(I32, (KEY_CHUNK, Q_BLOCK), 1)
            key_pos = c * KEY_CHUNK + lax.broadcasted_iota(I32, (KEY_CHUNK, Q_BLOCK), 0)
            s = jnp.where(key_pos <= q_pos, s, -jnp.inf)
        score_ref[chunk(c), :] = s

    n_acc = 4
    n_rows = n_chunks * KEY_CHUNK // SUBLANES

    def as_float(key):
        return lax.bitcast_convert_type(key ^ ((key >> 31) & 0x7FFFFFFF), F32)

    def count(compare, level):
        lvl = jnp.broadcast_to(level, (SUBLANES, Q_BLOCK))
        accs = [jnp.zeros((SUBLANES, Q_BLOCK), I32)] * n_acc
        for i in range(n_rows):
            hit = compare(score_ref[i * SUBLANES:(i + 1) * SUBLANES, :], lvl)
            accs[i % n_acc] = accs[i % n_acc] + jnp.where(hit, 1, 0)
        return jnp.sum((accs[0] + accs[1]) + (accs[2] + accs[3]), axis=0, keepdims=True)

    def bit_body(it, cand):
        trial = cand + jnp.left_shift(jnp.int32(1), 31 - it)
        return jnp.where(count(jnp.greater_equal, as_float(trial)) >= top_k, trial, cand)

    thr_key = lax.fori_loop(0, 32, bit_body, jnp.full((1, Q_BLOCK), INT_MIN, I32))
    thr = jnp.where(thr_key == INT_MIN, -jnp.inf, as_float(thr_key))
    need = jnp.where(thr == -jnp.inf, 0, top_k - count(jnp.greater, thr)).astype(F32)

    zeros_q = jnp.zeros((HEAD_DIM, wide), BF16)
    qt_groups = []
    for g in range(N_KV_HEADS):
        qg = jnp.concatenate([qt_ref[0, g * Q_PER_KV + r] for r in range(Q_PER_KV)], axis=1)
        qt_groups.append(jnp.concatenate([qg, zeros_q] if g == 0 else [zeros_q, qg], axis=0))
    half = KEY_CHUNK // 2
    tri_r = lax.broadcasted_iota(I32, (KEY_CHUNK, KEY_CHUNK), 0)
    tri_c = lax.broadcasted_iota(I32, (KEY_CHUNK, KEY_CHUNK), 1)
    prefix_mat = jnp.where(tri_c <= tri_r, 1.0, 0.0).astype(BF16)

    m_ref[...] = jnp.full(m_ref.shape, M_INIT, F32)
    acc_ref[...] = jnp.zeros(acc_ref.shape, F32)
    piece = KEY_CHUNK // ATTN_PIECES

    def mask_bias(c, ties_before):
        kc = score_ref[chunk(c), :]
        is_tie = kc == thr
        tie = jnp.where(is_tie, 1.0, 0.0).astype(BF16)
        rank_top = jnp.dot(prefix_mat[0:half, 0:half], tie[0:half], preferred_element_type=F32)
        rank_bot = jnp.dot(prefix_mat[half:KEY_CHUNK, :], tie, preferred_element_type=F32)
        tie_rank = ties_before + jnp.concatenate([rank_top, rank_bot], axis=0)
        admitted = jnp.where(is_tie, tie_rank, 2.0 * KEY_CHUNK * KEY_CHUNK) <= need
        bias = jnp.where(kc > thr, 0.0, jnp.where(admitted, 0.0, MASK_BIAS))
        return bias, tie_rank[KEY_CHUNK - 1:KEY_CHUNK, :]

    def raw_scores(c, g):
        s_ref[g] = jnp.dot(k_ref[0, chunk(c), :], qt_groups[g], preferred_element_type=F32)

    def softmax_pv(c, g, bias):
        mx = None
        for j in range(ATTN_PIECES):
            rows = slice(j * piece, (j + 1) * piece)
            s = s_ref[g, rows, :] + jnp.concatenate([bias[rows]] * Q_PER_KV, axis=1)
            s_ref[g, rows, :] = s
            pm = jnp.max(s.reshape(piece // SUBLANES, SUBLANES, wide), axis=0)
            mx = pm if mx is None else jnp.maximum(mx, pm)
        m_old = m_ref[g]
        m_new = jnp.maximum(m_old, jnp.max(mx, axis=0, keepdims=True))
        m_ref[g] = m_new
        for j in range(ATTN_PIECES):
            rows = slice(j * piece, (j + 1) * piece)
            p_ref[g, rows, :] = jnp.exp2(s_ref[g, rows, :] - m_new).astype(BF16)
        pv = jnp.dot(vta_ref[0, g, :, chunk(c)], p_ref[g], preferred_element_type=F32)
        acc_ref[g] = jnp.exp2(m_old - m_new) * acc_ref[g] + pv

    bias, ties = mask_bias(0, jnp.zeros((1, Q_BLOCK), F32))
    raw_scores(0, 0)
    for c in range(n_chunks):
        raw_scores(c, 1)
        softmax_pv(c, 0, bias)
        if c + 1 < n_chunks:
            bias_next, ties = mask_bias(c + 1, ties)
            raw_scores(c + 1, 0)
        softmax_pv(c, 1, bias)
        if c + 1 < n_chunks:
            bias = bias_next

    for g in range(N_KV_HEADS):
        acc = acc_ref[g]
        out = acc[0:HEAD_DIM, :] / acc[HEAD_DIM:HEAD_DIM + 1, :]
        for r in range(0, Q_PER_KV, 2):
            pair = jnp.concatenate([out[:, r * Q_BLOCK:(r + 1) * Q_BLOCK], out[:, (r + 1) * Q_BLOCK:(r + 2) * Q_BLOCK]],
                                   axis=0)
            col = (g * Q_PER_KV + r) * HEAD_DIM
            o_ref[0, :, col:col + 2 * HEAD_DIM] = pair.T.astype(o_ref.dtype)


def _dsa(qt, qit, wt, kib, k, vta, top_k):
    b, _, _, l = qt.shape
    wide = Q_PER_KV * Q_BLOCK
    per = KEY_CHUNK // Q_BLOCK
    outs = []
    for n_chunks in range(1, l // KEY_CHUNK + 1):
        first = (n_chunks - 1) * per
        keys = n_chunks * KEY_CHUNK
        outs.append(pl.pallas_call(
            functools.partial(_dsa_kernel, top_k, n_chunks),
            grid=(b, per),
            in_specs=[
                pl.BlockSpec((1, N_Q_HEADS, HEAD_DIM, Q_BLOCK), lambda bi, i, first=first: (bi, 0, 0, first + i)),
                pl.BlockSpec((1, IDX_HEADS, IDX_DIM, Q_BLOCK), lambda bi, i, first=first: (bi, 0, 0, first + i)),
                pl.BlockSpec((1, SUBLANES, Q_BLOCK), lambda bi, i, first=first: (bi, 0, first + i)),
                pl.BlockSpec((1, keys, LANES), lambda bi, i: (bi, 0, 0)),
                pl.BlockSpec((1, keys, LANES), lambda bi, i: (bi, 0, 0)),
                pl.BlockSpec((1, N_KV_HEADS, V_ROWS, keys), lambda bi, i: (bi, 0, 0, 0)),
            ],
            out_specs=pl.BlockSpec((1, Q_BLOCK, ATTN_WIDTH), lambda bi, i: (bi, i, 0)),
            out_shape=jax.ShapeDtypeStruct((b, KEY_CHUNK, ATTN_WIDTH), BF16),
            scratch_shapes=[
                pltpu.VMEM((keys, Q_BLOCK), F32),
                pltpu.VMEM((N_KV_HEADS, KEY_CHUNK, wide), F32),
                pltpu.VMEM((N_KV_HEADS, KEY_CHUNK, wide), BF16),
                pltpu.VMEM((N_KV_HEADS, 1, wide), F32),
                pltpu.VMEM((N_KV_HEADS, V_ROWS, wide), F32),
            ],
            compiler_params=pltpu.CompilerParams(dimension_semantics=("arbitrary", "arbitrary"),
                                                 vmem_limit_bytes=VMEM_LIMIT_BYTES),
            name=f"dsa{n_chunks}",
        )(qt, qit, wt, kib, k, vta))
    return jnp.concatenate(outs, axis=1)


def _rglru_kernel(rx_ref, rgate_ref, cw_ref, cb_ref, wg_ref, ba_ref, bx_ref, lam_ref, o_ref, xbuf_ref, h_ref):
    rows = rx_ref.shape[1]
    halo = SUBLANES

    @pl.when(pl.program_id(1) == 0)
    def _():
        xbuf_ref[0:halo, :] = jnp.zeros((halo, RNN_WIDTH), F32)
        h_ref[...] = jnp.zeros(h_ref.shape, F32)

    x = rx_ref[0].astype(F32)
    xbuf_ref[halo:halo + rows, :] = x
    groups = rows // SUBLANES
    sub = lax.broadcasted_iota(I32, (groups, SUBLANES, LANES), 1)

    for n in range(RNN_WIDTH // LANES):
        sl = slice(n * LANES, (n + 1) * LANES)
        xc = jnp.broadcast_to(cb_ref[:, sl], (rows, LANES))
        for j in range(CONV_WIDTH):
            start = halo - (CONV_WIDTH - 1) + j
            xc = xc + cw_ref[j:j + 1, sl] * xbuf_ref[start:start + rows, sl]
        gates = jnp.dot(xc.astype(BF16), wg_ref[n], preferred_element_type=F32)
        r = _sigmoid(gates[:, 0:LANES] + ba_ref[:, sl])
        i = _sigmoid(gates[:, LANES:2 * LANES] + bx_ref[:, sl])
        lam = lam_ref[:, sl]
        log_sig = jnp.minimum(lam, 0.0) - jnp.log1p(jnp.exp(-jnp.abs(lam)))
        log_a = LRU_C * r * log_sig
        a = jnp.exp(log_a)
        b = jnp.sqrt(1.0 - a * a) * (i * xc)
        a3 = a.reshape(groups, SUBLANES, LANES)
        b3 = b.reshape(groups, SUBLANES, LANES)
        d = 1
        while d < SUBLANES:
            live = sub >= d
            a_prev = jnp.where(live, pltpu.roll(a3, d, axis=1), 1.0)
            b_prev = jnp.where(live, pltpu.roll(b3, d, axis=1), 0.0)
            b3 = a3 * b_prev + b3
            a3 = a3 * a_prev
            d *= 2
        h_prev = h_ref[:, sl]
        hs = []
        for t in range(groups):
            h_t = a3[t] * h_prev + b3[t]
            hs.append(h_t)
            h_prev = h_t[SUBLANES - 1:SUBLANES, :]
        h_ref[:, sl] = h_prev
        h = jnp.concatenate(hs, axis=0)
        o_ref[0, :, sl] = (h * _gelu_tanh(rgate_ref[0, :, sl].astype(F32))).astype(o_ref.dtype)

    xbuf_ref[0:halo, :] = x[rows - halo:rows, :]


def _rglru(rx, rgate, cw, cb, wg, ba, bx, lam):
    b, l, _ = rx.shape
    rows = RNN_ROWS
    blk = lambda bi, i: (bi, i, 0)
    c2 = lambda bi, i: (0, 0)
    c3 = lambda bi, i: (0, 0, 0)
    return pl.pallas_call(
        _rglru_kernel,
        grid=(b, l // rows),
        in_specs=[
            pl.BlockSpec((1, rows, RNN_WIDTH), blk),
            pl.BlockSpec((1, rows, RNN_WIDTH), blk),
            pl.BlockSpec((CONV_WIDTH, RNN_WIDTH), c2),
            pl.BlockSpec((1, RNN_WIDTH), c2),
            pl.BlockSpec((RNN_WIDTH // LANES, LANES, 2 * LANES), c3),
            pl.BlockSpec((1, RNN_WIDTH), c2),
            pl.BlockSpec((1, RNN_WIDTH), c2),
            pl.BlockSpec((1, RNN_WIDTH), c2),
        ],
        out_specs=pl.BlockSpec((1, rows, RNN_WIDTH), blk),
        out_shape=jax.ShapeDtypeStruct((b, l, RNN_WIDTH), BF16),
        scratch_shapes=[
            pltpu.VMEM((rows + SUBLANES, RNN_WIDTH), F32),
            pltpu.VMEM((1, RNN_WIDTH), F32),
        ],
        compiler_params=pltpu.CompilerParams(dimension_semantics=("arbitrary", "arbitrary"),
                                             vmem_limit_bytes=VMEM_LIMIT_BYTES),
        name="rglru",
    )(rx, rgate, cw, cb, wg, ba, bx, lam)


def _merge_kernel(attn_ref, rnn_ref, ga_ref, gb_ref, x_ref, woa_ref, wor_ref, wout_ref, g2_ref, x1_ref, xn_ref):
    a = jnp.dot(attn_ref[...], woa_ref[...], preferred_element_type=F32)
    r = jnp.dot(rnn_ref[...], wor_ref[...], preferred_element_type=F32)
    merged = _sigmoid(ga_ref[...].astype(F32)) * a + _sigmoid(gb_ref[...].astype(F32)) * r
    x1 = x_ref[...] + jnp.dot(merged.astype(BF16), wout_ref[...], preferred_element_type=F32)
    x1_ref[...] = x1
    ms = jnp.mean(x1 * x1, axis=-1, keepdims=True)
    xn_ref[...] = (x1 * lax.rsqrt(ms + EPS) * g2_ref[...]).astype(BF16)


def _merge(attn, rnn, ga, gb, x2, woa, wor, wout, g2):
    m = x2.shape[0]
    rows = MERGE_ROWS
    row = lambda i: (i, 0)
    const = lambda i: (0, 0)
    return pl.pallas_call(
        _merge_kernel,
        grid=(m // rows,),
        in_specs=[
            pl.BlockSpec((rows, ATTN_WIDTH), row),
            pl.BlockSpec((rows, RNN_WIDTH), row),
            pl.BlockSpec((rows, D_MODEL), row),
            pl.BlockSpec((rows, D_MODEL), row),
            pl.BlockSpec((rows, D_MODEL), row),
            pl.BlockSpec((ATTN_WIDTH, D_MODEL), const, pipeline_mode=pl.Buffered(1)),
            pl.BlockSpec((RNN_WIDTH, D_MODEL), const, pipeline_mode=pl.Buffered(1)),
            pl.BlockSpec((D_MODEL, D_MODEL), const, pipeline_mode=pl.Buffered(1)),
            pl.BlockSpec((1, D_MODEL), const),
        ],
        out_specs=[pl.BlockSpec((rows, D_MODEL), row), pl.BlockSpec((rows, D_MODEL), row)],
        out_shape=[jax.ShapeDtypeStruct((m, D_MODEL), F32), jax.ShapeDtypeStruct((m, D_MODEL), BF16)],
        compiler_params=pltpu.CompilerParams(dimension_semantics=("arbitrary",), vmem_limit_bytes=VMEM_LIMIT_BYTES),
        name="merge",
    )(attn, rnn, ga, gb, x2, woa, wor, wout, g2)


def _ffn_kernel(xn_ref, x1_ref, wup_ref, cw_ref, cb_ref, wdn_ref, o_ref, tail_ref, hid_ref):
    rows = xn_ref.shape[1]

    @pl.when(pl.program_id(1) == 0)
    def _():
        tail_ref[...] = jnp.zeros(tail_ref.shape, F32)

    xn = xn_ref[0]
    sub = lax.broadcasted_iota(I32, (SUBLANES, FFN_CHUNK), 0)

    def conv(col0):
        cols = slice(col0, col0 + FFN_CHUNK)
        u = jnp.dot(xn, wup_ref[:, cols], preferred_element_type=F32)
        cw = cw_ref[:, cols]
        cb = cb_ref[:, cols]
        taps = lambda u2, u1, u0: cb + cw[0:1] * u2 + cw[1:2] * u1 + cw[2:3] * u0
        y = taps(pltpu.roll(u, 2, axis=0), pltpu.roll(u, 1, axis=0), u)
        tail = tail_ref[:, cols]
        head = u[0:SUBLANES]
        prev1 = tail[SUBLANES - 1:SUBLANES]
        prev2 = tail[SUBLANES - 2:SUBLANES - 1]
        h1 = jnp.where(sub == 0, prev1, pltpu.roll(head, 1, axis=0))
        h2 = jnp.where(sub == 0, prev2, jnp.where(sub == 1, prev1, pltpu.roll(head, 2, axis=0)))
        tail_ref[:, cols] = u[rows - SUBLANES:rows]
        return jnp.concatenate([taps(h2, h1, head), y[SUBLANES:]], axis=0)

    for j in range(D_FF // FFN_CHUNK):
        gate = conv(j * FFN_CHUNK)
        val = conv(D_FF + j * FFN_CHUNK)
        hid_ref[:, j * FFN_CHUNK:(j + 1) * FFN_CHUNK] = (gate * _sigmoid(gate) * val).astype(BF16)
    o_ref[0] = x1_ref[0] + jnp.dot(hid_ref[...], wdn_ref[...], preferred_element_type=F32)


def _ffn(xn, x1, wup, cw, cb, wdn):
    b, l, _ = xn.shape
    rows = FFN_ROWS
    blk = lambda bi, i: (bi, i, 0)
    const = lambda bi, i: (0, 0)
    return pl.pallas_call(
        _ffn_kernel,
        grid=(b, l // rows),
        in_specs=[
            pl.BlockSpec((1, rows, D_MODEL), blk),
            pl.BlockSpec((1, rows, D_MODEL), blk),
            pl.BlockSpec((D_MODEL, 2 * D_FF), const, pipeline_mode=pl.Buffered(1)),
            pl.BlockSpec((FFN_CONV_WIDTH, 2 * D_FF), const),
            pl.BlockSpec((1, 2 * D_FF), const),
            pl.BlockSpec((D_FF, D_MODEL), const, pipeline_mode=pl.Buffered(1)),
        ],
        out_specs=pl.BlockSpec((1, rows, D_MODEL), blk),
        out_shape=jax.ShapeDtypeStruct((b, l, D_MODEL), F32),
        scratch_shapes=[
            pltpu.VMEM((SUBLANES, 2 * D_FF), F32),
            pltpu.VMEM((rows, D_FF), BF16),
        ],
        compiler_params=pltpu.CompilerParams(dimension_semantics=("arbitrary", "arbitrary"),
                                             vmem_limit_bytes=VMEM_LIMIT_BYTES),
        name="ffn",
    )(xn, x1, wup, cw, cb, wdn)


def _block_diag_pairs(w):
    z = jnp.zeros((RNN_BLOCK_DIM, RNN_BLOCK_DIM), w.dtype)
    w = w.reshape(RNN_BLOCKS // 2, 2, RNN_BLOCK_DIM, RNN_BLOCK_DIM)
    top = jnp.concatenate([w[:, 0], jnp.broadcast_to(z, w[:, 0].shape)], axis=-1)
    bot = jnp.concatenate([jnp.broadcast_to(z, w[:, 1].shape), w[:, 1]], axis=-1)
    return jnp.concatenate([top, bot], axis=-2)


def _layer(x, top_k, norm1_g, w_in, q_norm_g, k_norm_g, kidx_norm_g, conv_w, conv_b, rg_wa, rg_ba, rg_wx, rg_bx,
           rg_lambda, w_o_attn, w_o_rnn, w_out, norm2_g, w_up, ffn_conv_w, ffn_conv_b, w_down):
    b, l, _ = x.shape
    m = b * l
    x2 = x.reshape(m, D_MODEL)

    n_head = ATTN_WIDTH + 2 * KV_WIDTH + IDX_HEADS * IDX_DIM + IDX_DIM + IDX_HEADS
    pad = jnp.zeros((D_MODEL, ATTN_SLAB - n_head), w_in.dtype)
    w_all = jnp.concatenate([w_in[:, :n_head], pad, w_in[:, n_head:]], axis=1).astype(BF16)
    pair = lambda g: jnp.tile(g, 2).reshape(1, LANES)
    kig = jnp.concatenate([kidx_norm_g, jnp.zeros((LANES - IDX_DIM,), F32)]).reshape(1, LANES)

    qt, qit, wt, kib, k, vta, rx, rgate, ga, gb = _proj(x2, norm1_g.reshape(1, D_MODEL), w_all,
                                                          pair(q_norm_g), pair(k_norm_g), kig, b, l)
    attn = _dsa(qt, qit, wt, kib, k, vta, top_k)

    wg = jnp.concatenate([_block_diag_pairs(rg_wa), _block_diag_pairs(rg_wx)], axis=-1).astype(BF16)
    rnn = _rglru(rx.reshape(b, l, RNN_WIDTH), rgate.reshape(b, l, RNN_WIDTH), conv_w, conv_b.reshape(1, -1), wg,
                 rg_ba.reshape(1, -1), rg_bx.reshape(1, -1), rg_lambda.reshape(1, -1))

    x1, xn2 = _merge(attn.reshape(m, ATTN_WIDTH), rnn.reshape(m, RNN_WIDTH), ga, gb, x2,
                     w_o_attn.astype(BF16), w_o_rnn.astype(BF16), w_out.astype(BF16), norm2_g.reshape(1, D_MODEL))

    out = _ffn(xn2.reshape(b, l, D_MODEL), x1.reshape(b, l, D_MODEL), w_up.astype(BF16), ffn_conv_w,
               ffn_conv_b.reshape(1, -1), w_down.astype(BF16))
    return out


def kernel(x, norm1_g, w_in, q_norm_g, k_norm_g, kidx_norm_g, conv_w, conv_b, rg_wa, rg_ba, rg_wx, rg_bx, rg_lambda,
           w_o_attn, w_o_rnn, w_out, norm2_g, w_up, ffn_conv_w, ffn_conv_b, w_down):
    l = x.shape[1]
    top_k = min(TOPK_MAX, l // 4)
    h = x
    for d in range(norm1_g.shape[0]):
        h = _layer(h, top_k, norm1_g[d], w_in[d], q_norm_g[d], k_norm_g[d], kidx_norm_g[d], conv_w[d], conv_b[d],
                   rg_wa[d], rg_ba[d], rg_wx[d], rg_bx[d], rg_lambda[d], w_o_attn[d], w_o_rnn[d], w_out[d],
                   norm2_g[d], w_up[d], ffn_conv_w[d], ffn_conv_b[d], w_down[d])
    return h
```

```python
import functools

import jax
import jax.numpy as jnp
import numpy as np
from jax import lax
from jax.experimental import pallas as pl
from jax.experimental.pallas import tpu as pltpu

F32 = jnp.float32
BF16 = jnp.bfloat16
I32 = jnp.int32

D_MODEL = 1024
HEAD_DIM = 64
N_Q_HEADS = 8
N_KV_HEADS = 2
Q_PER_KV = N_Q_HEADS // N_KV_HEADS
ATTN_WIDTH = N_Q_HEADS * HEAD_DIM
KV_WIDTH = N_KV_HEADS * HEAD_DIM
IDX_HEADS = 4
IDX_DIM = 64
TOPK_MAX = 256
RNN_WIDTH = D_MODEL
RNN_BLOCKS = 16
RNN_BLOCK_DIM = RNN_WIDTH // RNN_BLOCKS
CONV_WIDTH = 4
LRU_C = 8.0
D_FF = 2816
FFN_CONV_WIDTH = 3
EPS = 1e-6

LANES = 128
SUBLANES = 8
PACK = 16
VMEM_LIMIT_BYTES = 56 * 1024 * 1024

PROJ_ROWS = 512
ATTN_SLAB = ATTN_WIDTH + 2 * KV_WIDTH + IDX_HEADS * IDX_DIM + LANES
Q_BLOCK = 256
KEY_CHUNK = 512
RNN_ROWS = 256
MERGE_ROWS = 512
FFN_ROWS = 512
FFN_CHUNK = 256

ATTN_PIECES = 4
ONES_ROWS = 16
V_ROWS = HEAD_DIM + ONES_ROWS

INT_MIN = -2 ** 31
LOG2E = 1.4426950408889634
MASK_BIAS = -1e30
M_INIT = -3e38


def _sigmoid(x):
    return 1.0 / (1.0 + jnp.exp(-x))


def _gelu_tanh(x):
    c = np.sqrt(2.0 / np.pi).astype(np.float32)
    return x * (0.5 * (1.0 + jnp.tanh(c * (x + 0.044715 * (x * x * x)))))


def _pair_norm(slab, gain):
    lane = lax.broadcasted_iota(I32, slab.shape, 1)
    lo = lane < HEAD_DIM
    sq = slab * slab
    s_lo = jnp.sum(jnp.where(lo, sq, 0.0), axis=-1, keepdims=True)
    s_hi = jnp.sum(jnp.where(lo, 0.0, sq), axis=-1, keepdims=True)
    inv = jnp.where(lo, lax.rsqrt(s_lo * (1.0 / HEAD_DIM) + EPS), lax.rsqrt(s_hi * (1.0 / HEAD_DIM) + EPS))
    return slab * inv * gain


def _proj_kernel(x_ref, g1_ref, w_ref, qg_ref, kg_ref, kig_ref,
                 qt_ref, qit_ref, wt_ref, kib_ref, k_ref, vta_ref, rx_ref, rgate_ref, ga_ref, gb_ref):
    x = x_ref[...]
    rows = x.shape[0]
    ms = jnp.mean(x * x, axis=-1, keepdims=True)
    xn = (x * lax.rsqrt(ms + EPS) * g1_ref[...]).astype(BF16)

    pa = jnp.dot(xn, w_ref[:, 0:ATTN_SLAB], preferred_element_type=F32)
    qg = qg_ref[...]
    kg = kg_ref[...]

    def put_heads(ref, first_head, slab):
        t = slab.T
        ref[0, first_head] = t[0:HEAD_DIM].astype(ref.dtype)
        ref[0, first_head + 1] = t[HEAD_DIM:2 * HEAD_DIM].astype(ref.dtype)

    scale = (HEAD_DIM ** -0.5) * LOG2E
    for j in range(ATTN_WIDTH // LANES):
        sl = slice(j * LANES, (j + 1) * LANES)
        put_heads(qt_ref, 2 * j, _pair_norm(pa[:, sl], qg) * scale)
    off = ATTN_WIDTH
    k_ref[0] = _pair_norm(pa[:, off:off + KV_WIDTH], kg).astype(BF16)
    off += KV_WIDTH
    put_heads(vta_ref.at[:, :, 0:HEAD_DIM, :], 0, pa[:, off:off + KV_WIDTH])
    ones = jnp.ones((ONES_ROWS, rows), BF16)
    for g in range(N_KV_HEADS):
        vta_ref[0, g, HEAD_DIM:HEAD_DIM + ONES_ROWS, :] = ones
    off += KV_WIDTH
    for j in range(IDX_HEADS * IDX_DIM // LANES):
        put_heads(qit_ref, 2 * j, pa[:, off + j * LANES:off + (j + 1) * LANES] * (IDX_DIM ** -0.5))
    off += IDX_HEADS * IDX_DIM
    tail = pa[:, off:off + LANES]
    lane = lax.broadcasted_iota(I32, tail.shape, 1)
    is_ki = lane < IDX_DIM
    s_ki = jnp.sum(jnp.where(is_ki, tail * tail, 0.0), axis=-1, keepdims=True)
    ki_n = tail * lax.rsqrt(s_ki * (1.0 / IDX_DIM) + EPS) * kig_ref[...]
    kiwi = jnp.where(is_ki, ki_n, tail * (IDX_HEADS ** -0.5))
    kib_ref[0] = kiwi.astype(BF16)
    wt_ref[0] = kiwi.T[IDX_DIM:IDX_DIM + SUBLANES]

    off = ATTN_SLAB
    for ref in (rx_ref, rgate_ref, ga_ref, gb_ref):
        ref[...] = jnp.dot(xn, w_ref[:, off:off + D_MODEL], preferred_element_type=F32).astype(BF16)
        off += D_MODEL


def _proj(x2, g1, w_all, qg, kg, kig, b, l):
    m = x2.shape[0]
    rows = PROJ_ROWS
    tpb = l // rows
    const = lambda i: (0, 0)
    row = lambda i: (i, 0)
    n_all = w_all.shape[1]
    head_major = lambda i: (i // tpb, 0, 0, i % tpb)
    seq_major = lambda i: (i // tpb, i % tpb, 0)
    wide = [(D_MODEL, BF16)] * 4
    return pl.pallas_call(
        _proj_kernel,
        grid=(m // rows,),
        in_specs=[
            pl.BlockSpec((rows, D_MODEL), row),
            pl.BlockSpec((1, D_MODEL), const),
            pl.BlockSpec((D_MODEL, n_all), const, pipeline_mode=pl.Buffered(1)),
            pl.BlockSpec((1, LANES), const),
            pl.BlockSpec((1, LANES), const),
            pl.BlockSpec((1, LANES), const),
        ],
        out_specs=[
            pl.BlockSpec((1, N_Q_HEADS, HEAD_DIM, rows), head_major),
            pl.BlockSpec((1, IDX_HEADS, IDX_DIM, rows), head_major),
            pl.BlockSpec((1, SUBLANES, rows), lambda i: (i // tpb, 0, i % tpb)),
            pl.BlockSpec((1, rows, LANES), seq_major),
            pl.BlockSpec((1, rows, LANES), seq_major),
            pl.BlockSpec((1, N_KV_HEADS, V_ROWS, rows), head_major),
        ] + [pl.BlockSpec((rows, w), row) for w, _ in wide],
        out_shape=[
            jax.ShapeDtypeStruct((b, N_Q_HEADS, HEAD_DIM, l), BF16),
            jax.ShapeDtypeStruct((b, IDX_HEADS, IDX_DIM, l), BF16),
            jax.ShapeDtypeStruct((b, SUBLANES, l), F32),
            jax.ShapeDtypeStruct((b, l, LANES), BF16),
            jax.ShapeDtypeStruct((b, l, LANES), BF16),
            jax.ShapeDtypeStruct((b, N_KV_HEADS, V_ROWS, l), BF16),
        ] + [jax.ShapeDtypeStruct((m, w), dt) for w, dt in wide],
        compiler_params=pltpu.CompilerParams(dimension_semantics=("arbitrary",), vmem_limit_bytes=VMEM_LIMIT_BYTES),
        name="proj",
    )(x2, g1, w_all, qg, kg, kig)


def _dsa_kernel(top_k, n_chunks, qt_ref, qit_ref, wt_ref, kib_ref, k_ref, vta_ref, o_ref,
                score_ref, coarse_ref, s_ref, p_ref, m_ref, acc_ref):
    blk = (n_chunks - 1) * (KEY_CHUNK // Q_BLOCK) + pl.program_id(1)
    packed = KEY_CHUNK // PACK
    wide = Q_PER_KV * Q_BLOCK
    chunk = lambda c: slice(c * KEY_CHUNK, (c + 1) * KEY_CHUNK)

    qit = jnp.concatenate([qit_ref[0, h] for h in range(IDX_HEADS)], axis=1)
    qit = jnp.concatenate([qit, jnp.zeros_like(qit)], axis=0)
    wt = wt_ref[0]

    for c in range(n_chunks):
        d = jnp.dot(kib_ref[0, chunk(c), :], qit, preferred_element_type=F32)
        s = jnp.zeros((KEY_CHUNK, Q_BLOCK), F32)
        for h in range(IDX_HEADS):
            s = s + wt[h:h + 1, :] * jnp.maximum(d[:, h * Q_BLOCK:(h + 1) * Q_BLOCK], 0.0)
        if c == n_chunks - 1:
            q_pos = blk * Q_BLOCK + lax.broadcasted_iota(I32, (KEY_CHUNK, Q_BLOCK), 1)
            key_pos = c * KEY_CHUNK + lax.broadcasted_iota(I32, (KEY_CHUNK, Q_BLOCK), 0)
            s = jnp.where(key_pos <= q_pos, s, -jnp.inf)
        score_ref[chunk(c), :] = s
        coarse_ref[c * packed:(c + 1) * packed] = s.astype(BF16).reshape(packed, PACK, Q_BLOCK)

    n_acc = 4

    def order_flip(key):
        return key ^ ((key >> 31) & 0x7FFFFFFF)

    def as_float(key):
        return lax.bitcast_convert_type(order_flip(key), F32)

    one = jnp.ones((PACK, Q_BLOCK), BF16)
    zero = jnp.zeros((PACK, Q_BLOCK), BF16)

    def count_coarse(level):
        lvl = jnp.broadcast_to(level.astype(BF16), (PACK, Q_BLOCK))
        accs = [zero] * n_acc
        for i in range(n_chunks * packed):
            accs[i % n_acc] = accs[i % n_acc] + jnp.where(coarse_ref[i] >= lvl, one, zero)
        tot = (accs[0].astype(F32) + accs[1].astype(F32)) + (accs[2].astype(F32) + accs[3].astype(F32))
        return jnp.sum(tot, axis=0, keepdims=True)

    def count(compare, level):
        lvl = jnp.broadcast_to(level, (SUBLANES, Q_BLOCK))
        accs = [jnp.zeros((SUBLANES, Q_BLOCK), I32)] * n_acc
        for i in range(n_chunks * KEY_CHUNK // SUBLANES):
            hit = compare(score_ref[i * SUBLANES:(i + 1) * SUBLANES, :], lvl)
            accs[i % n_acc] = accs[i % n_acc] + jnp.where(hit, 1, 0)
        return jnp.sum((accs[0] + accs[1]) + (accs[2] + accs[3]), axis=0, keepdims=True)

    def coarse_body(it, cand):
        trial = cand + jnp.left_shift(jnp.int32(1), 31 - it)
        level = lax.bitcast_convert_type(order_flip(trial) & -65536, F32)
        return jnp.where(count_coarse(level) >= top_k, trial, cand)

    coarse = lax.fori_loop(0, 16, coarse_body, jnp.full((1, Q_BLOCK), INT_MIN, I32))
    centre = order_flip(order_flip(coarse) & -65536)
    base = centre - 32768

    def fine_body(it, offset):
        trial = offset + jnp.left_shift(jnp.int32(1), 16 - it)
        return jnp.where(count(jnp.greater_equal, as_float(base + trial)) >= top_k, trial, offset)

    thr = as_float(base + lax.fori_loop(0, 17, fine_body, jnp.zeros((1, Q_BLOCK), I32)))
    thr = jnp.where(coarse == INT_MIN, -jnp.inf, thr)
    need = jnp.where(thr == -jnp.inf, 0, top_k - count(jnp.greater, thr)).astype(F32)

    zeros_q = jnp.zeros((HEAD_DIM, wide), BF16)
    qt_groups = []
    for g in range(N_KV_HEADS):
        qg = jnp.concatenate([qt_ref[0, g * Q_PER_KV + r] for r in range(Q_PER_KV)], axis=1)
        qt_groups.append(jnp.concatenate([qg, zeros_q] if g == 0 else [zeros_q, qg], axis=0))
    half = KEY_CHUNK // 2
    tri_r = lax.broadcasted_iota(I32, (KEY_CHUNK, KEY_CHUNK), 0)
    tri_c = lax.broadcasted_iota(I32, (KEY_CHUNK, KEY_CHUNK), 1)
    prefix_mat = jnp.where(tri_c <= tri_r, 1.0, 0.0).astype(BF16)

    m_ref[...] = jnp.full(m_ref.shape, M_INIT, F32)
    acc_ref[...] = jnp.zeros(acc_ref.shape, F32)
    piece = KEY_CHUNK // ATTN_PIECES

    def mask_bias(c, ties_before):
        kc = score_ref[chunk(c), :]
        is_tie = kc == thr
        tie = jnp.where(is_tie, 1.0, 0.0).astype(BF16)
        rank_top = jnp.dot(prefix_mat[0:half, 0:half], tie[0:half], preferred_element_type=F32)
        rank_bot = jnp.dot(prefix_mat[half:KEY_CHUNK, :], tie, preferred_element_type=F32)
        tie_rank = ties_before + jnp.concatenate([rank_top, rank_bot], axis=0)
        admitted = jnp.where(is_tie, tie_rank, 2.0 * KEY_CHUNK * KEY_CHUNK) <= need
        bias = jnp.where(kc > thr, 0.0, jnp.where(admitted, 0.0, MASK_BIAS))
        return bias, tie_rank[KEY_CHUNK - 1:KEY_CHUNK, :]

    def raw_scores(c, g):
        s_ref[g] = jnp.dot(k_ref[0, chunk(c), :], qt_groups[g], preferred_element_type=F32)

    def softmax_pv(c, g, bias):
        mx = None
        for j in range(ATTN_PIECES):
            rows = slice(j * piece, (j + 1) * piece)
            s = s_ref[g, rows, :] + jnp.concatenate([bias[rows]] * Q_PER_KV, axis=1)
            s_ref[g, rows, :] = s
            pm = jnp.max(s.reshape(piece // SUBLANES, SUBLANES, wide), axis=0)
            mx = pm if mx is None else jnp.maximum(mx, pm)
        m_old = m_ref[g]
        m_new = jnp.maximum(m_old, jnp.max(mx, axis=0, keepdims=True))
        m_ref[g] = m_new
        for j in range(ATTN_PIECES):
            rows = slice(j * piece, (j + 1) * piece)
            p_ref[g, rows, :] = jnp.exp2(s_ref[g, rows, :] - m_new).astype(BF16)
        pv = jnp.dot(vta_ref[0, g, :, chunk(c)], p_ref[g], preferred_element_type=F32)
        acc_ref[g] = jnp.exp2(m_old - m_new) * acc_ref[g] + pv

    bias, ties = mask_bias(0, jnp.zeros((1, Q_BLOCK), F32))
    raw_scores(0, 0)
    for c in range(n_chunks):
        raw_scores(c, 1)
        softmax_pv(c, 0, bias)
        if c + 1 < n_chunks:
            bias_next, ties = mask_bias(c + 1, ties)
            raw_scores(c + 1, 0)
        softmax_pv(c, 1, bias)
        if c + 1 < n_chunks:
            bias = bias_next

    for g in range(N_KV_HEADS):
        acc = acc_ref[g]
        out = acc[0:HEAD_DIM, :] / acc[HEAD_DIM:HEAD_DIM + 1, :]
        for r in range(0, Q_PER_KV, 2):
            pair = jnp.concatenate([out[:, r * Q_BLOCK:(r + 1) * Q_BLOCK], out[:, (r + 1) * Q_BLOCK:(r + 2) * Q_BLOCK]],
                                   axis=0)
            col = (g * Q_PER_KV + r) * HEAD_DIM
            o_ref[0, :, col:col + 2 * HEAD_DIM] = pair.T.astype(o_ref.dtype)


def _dsa(qt, qit, wt, kib, k, vta, top_k):
    b, _, _, l = qt.shape
    assert l // PACK <= 256, "bf16 counters hold at most 256 keys per slot"
    wide = Q_PER_KV * Q_BLOCK
    per = KEY_CHUNK // Q_BLOCK
    outs = []
    for n_chunks in range(1, l // KEY_CHUNK + 1):
        first = (n_chunks - 1) * per
        keys = n_chunks * KEY_CHUNK
        outs.append(pl.pallas_call(
            functools.partial(_dsa_kernel, top_k, n_chunks),
            grid=(b, per),
            in_specs=[
                pl.BlockSpec((1, N_Q_HEADS, HEAD_DIM, Q_BLOCK), lambda bi, i, first=first: (bi, 0, 0, first + i)),
                pl.BlockSpec((1, IDX_HEADS, IDX_DIM, Q_BLOCK), lambda bi, i, first=first: (bi, 0, 0, first + i)),
                pl.BlockSpec((1, SUBLANES, Q_BLOCK), lambda bi, i, first=first: (bi, 0, first + i)),
                pl.BlockSpec((1, keys, LANES), lambda bi, i: (bi, 0, 0)),
                pl.BlockSpec((1, keys, LANES), lambda bi, i: (bi, 0, 0)),
                pl.BlockSpec((1, N_KV_HEADS, V_ROWS, keys), lambda bi, i: (bi, 0, 0, 0)),
            ],
            out_specs=pl.BlockSpec((1, Q_BLOCK, ATTN_WIDTH), lambda bi, i: (bi, i, 0)),
            out_shape=jax.ShapeDtypeStruct((b, KEY_CHUNK, ATTN_WIDTH), BF16),
            scratch_shapes=[
                pltpu.VMEM((keys, Q_BLOCK), F32),
                pltpu.VMEM((keys // PACK, PACK, Q_BLOCK), BF16),
                pltpu.VMEM((N_KV_HEADS, KEY_CHUNK, wide), F32),
                pltpu.VMEM((N_KV_HEADS, KEY_CHUNK, wide), BF16),
                pltpu.VMEM((N_KV_HEADS, 1, wide), F32),
                pltpu.VMEM((N_KV_HEADS, V_ROWS, wide), F32),
            ],
            compiler_params=pltpu.CompilerParams(dimension_semantics=("arbitrary", "arbitrary"),
                                                 vmem_limit_bytes=VMEM_LIMIT_BYTES),
            name=f"dsa{n_chunks}",
        )(qt, qit, wt, kib, k, vta))
    return jnp.concatenate(outs, axis=1)


def _rglru_kernel(rx_ref, rgate_ref, cw_ref, cb_ref, wg_ref, ba_ref, bx_ref, lam_ref, o_ref, xbuf_ref, h_ref):
    rows = rx_ref.shape[1]
    halo = SUBLANES

    @pl.when(pl.program_id(1) == 0)
    def _():
        xbuf_ref[0:halo, :] = jnp.zeros((halo, RNN_WIDTH), F32)
        h_ref[...] = jnp.zeros(h_ref.shape, F32)

    x = rx_ref[0].astype(F32)
    xbuf_ref[halo:halo + rows, :] = x
    groups = rows // SUBLANES
    sub = lax.broadcasted_iota(I32, (groups, SUBLANES, LANES), 1)

    for n in range(RNN_WIDTH // LANES):
        sl = slice(n * LANES, (n + 1) * LANES)
        xc = jnp.broadcast_to(cb_ref[:, sl], (rows, LANES))
        for j in range(CONV_WIDTH):
            start = halo - (CONV_WIDTH - 1) + j
            xc = xc + cw_ref[j:j + 1, sl] * xbuf_ref[start:start + rows, sl]
        gates = jnp.dot(xc.astype(BF16), wg_ref[n], preferred_element_type=F32)
        r = _sigmoid(gates[:, 0:LANES] + ba_ref[:, sl])
        i = _sigmoid(gates[:, LANES:2 * LANES] + bx_ref[:, sl])
        lam = lam_ref[:, sl]
        log_sig = jnp.minimum(lam, 0.0) - jnp.log1p(jnp.exp(-jnp.abs(lam)))
        log_a = LRU_C * r * log_sig
        a = jnp.exp(log_a)
        b = jnp.sqrt(1.0 - a * a) * (i * xc)
        a3 = a.reshape(groups, SUBLANES, LANES)
        b3 = b.reshape(groups, SUBLANES, LANES)
        d = 1
        while d < SUBLANES:
            live = sub >= d
            a_prev = jnp.where(live, pltpu.roll(a3, d, axis=1), 1.0)
            b_prev = jnp.where(live, pltpu.roll(b3, d, axis=1), 0.0)
            b3 = a3 * b_prev + b3
            a3 = a3 * a_prev
            d *= 2
        h_prev = h_ref[:, sl]
        hs = []
        for t in range(groups):
            h_t = a3[t] * h_prev + b3[t]
            hs.append(h_t)
            h_prev = h_t[SUBLANES - 1:SUBLANES, :]
        h_ref[:, sl] = h_prev
        h = jnp.concatenate(hs, axis=0)
        o_ref[0, :, sl] = (h * _gelu_tanh(rgate_ref[0, :, sl].astype(F32))).astype(o_ref.dtype)

    xbuf_ref[0:halo, :] = x[rows - halo:rows, :]


def _rglru(rx, rgate, cw, cb, wg, ba, bx, lam):
    b, l, _ = rx.shape
    rows = RNN_ROWS
    blk = lambda bi, i: (bi, i, 0)
    c2 = lambda bi, i: (0, 0)
    c3 = lambda bi, i: (0, 0, 0)
    return pl.pallas_call(
        _rglru_kernel,
        grid=(b, l // rows),
        in_specs=[
            pl.BlockSpec((1, rows, RNN_WIDTH), blk),
            pl.BlockSpec((1, rows, RNN_WIDTH), blk),
            pl.BlockSpec((CONV_WIDTH, RNN_WIDTH), c2),
            pl.BlockSpec((1, RNN_WIDTH), c2),
            pl.BlockSpec((RNN_WIDTH // LANES, LANES, 2 * LANES), c3),
            pl.BlockSpec((1, RNN_WIDTH), c2),
            pl.BlockSpec((1, RNN_WIDTH), c2),
            pl.BlockSpec((1, RNN_WIDTH), c2),
        ],
        out_specs=pl.BlockSpec((1, rows, RNN_WIDTH), blk),
        out_shape=jax.ShapeDtypeStruct((b, l, RNN_WIDTH), BF16),
        scratch_shapes=[
            pltpu.VMEM((rows + SUBLANES, RNN_WIDTH), F32),
            pltpu.VMEM((1, RNN_WIDTH), F32),
        ],
        compiler_params=pltpu.CompilerParams(dimension_semantics=("arbitrary", "arbitrary"),
                                             vmem_limit_bytes=VMEM_LIMIT_BYTES),
        name="rglru",
    )(rx, rgate, cw, cb, wg, ba, bx, lam)


def _merge_kernel(attn_ref, rnn_ref, ga_ref, gb_ref, x_ref, woa_ref, wor_ref, wout_ref, g2_ref, x1_ref, xn_ref):
    a = jnp.dot(attn_ref[...], woa_ref[...], preferred_element_type=F32)
    r = jnp.dot(rnn_ref[...], wor_ref[...], preferred_element_type=F32)
    merged = _sigmoid(ga_ref[...].astype(F32)) * a + _sigmoid(gb_ref[...].astype(F32)) * r
    x1 = x_ref[...] + jnp.dot(merged.astype(BF16), wout_ref[...], preferred_element_type=F32)
    x1_ref[...] = x1
    ms = jnp.mean(x1 * x1, axis=-1, keepdims=True)
    xn_ref[...] = (x1 * lax.rsqrt(ms + EPS) * g2_ref[...]).astype(BF16)


def _merge(attn, rnn, ga, gb, x2, woa, wor, wout, g2):
    m = x2.shape[0]
    rows = MERGE_ROWS
    row = lambda i: (i, 0)
    const = lambda i: (0, 0)
    return pl.pallas_call(
        _merge_kernel,
        grid=(m // rows,),
        in_specs=[
            pl.BlockSpec((rows, ATTN_WIDTH), row),
            pl.BlockSpec((rows, RNN_WIDTH), row),
            pl.BlockSpec((rows, D_MODEL), row),
            pl.BlockSpec((rows, D_MODEL), row),
            pl.BlockSpec((rows, D_MODEL), row),
            pl.BlockSpec((ATTN_WIDTH, D_MODEL), const, pipeline_mode=pl.Buffered(1)),
            pl.BlockSpec((RNN_WIDTH, D_MODEL), const, pipeline_mode=pl.Buffered(1)),
            pl.BlockSpec((D_MODEL, D_MODEL), const, pipeline_mode=pl.Buffered(1)),
            pl.BlockSpec((1, D_MODEL), const),
        ],
        out_specs=[pl.BlockSpec((rows, D_MODEL), row), pl.BlockSpec((rows, D_MODEL), row)],
        out_shape=[jax.ShapeDtypeStruct((m, D_MODEL), F32), jax.ShapeDtypeStruct((m, D_MODEL), BF16)],
        compiler_params=pltpu.CompilerParams(dimension_semantics=("arbitrary",), vmem_limit_bytes=VMEM_LIMIT_BYTES),
        name="merge",
    )(attn, rnn, ga, gb, x2, woa, wor, wout, g2)


def _ffn_kernel(xn_ref, x1_ref, wup_ref, cw_ref, cb_ref, wdn_ref, o_ref, tail_ref, hid_ref):
    rows = xn_ref.shape[1]

    @pl.when(pl.program_id(1) == 0)
    def _():
        tail_ref[...] = jnp.zeros(tail_ref.shape, F32)

    xn = xn_ref[0]
    sub = lax.broadcasted_iota(I32, (SUBLANES, FFN_CHUNK), 0)

    def conv(col0):
        cols = slice(col0, col0 + FFN_CHUNK)
        u = jnp.dot(xn, wup_ref[:, cols], preferred_element_type=F32)
        cw = cw_ref[:, cols]
        cb = cb_ref[:, cols]
        taps = lambda u2, u1, u0: cb + cw[0:1] * u2 + cw[1:2] * u1 + cw[2:3] * u0
        y = taps(pltpu.roll(u, 2, axis=0), pltpu.roll(u, 1, axis=0), u)
        tail = tail_ref[:, cols]
        head = u[0:SUBLANES]
        prev1 = tail[SUBLANES - 1:SUBLANES]
        prev2 = tail[SUBLANES - 2:SUBLANES - 1]
        h1 = jnp.where(sub == 0, prev1, pltpu.roll(head, 1, axis=0))
        h2 = jnp.where(sub == 0, prev2, jnp.where(sub == 1, prev1, pltpu.roll(head, 2, axis=0)))
        tail_ref[:, cols] = u[rows - SUBLANES:rows]
        return jnp.concatenate([taps(h2, h1, head), y[SUBLANES:]], axis=0)

    for j in range(D_FF // FFN_CHUNK):
        gate = conv(j * FFN_CHUNK)
        val = conv(D_FF + j * FFN_CHUNK)
        hid_ref[:, j * FFN_CHUNK:(j + 1) * FFN_CHUNK] = (gate * _sigmoid(gate) * val).astype(BF16)
    o_ref[0] = x1_ref[0] + jnp.dot(hid_ref[...], wdn_ref[...], preferred_element_type=F32)


def _ffn(xn, x1, wup, cw, cb, wdn):
    b, l, _ = xn.shape
    rows = FFN_ROWS
    blk = lambda bi, i: (bi, i, 0)
    const = lambda bi, i: (0, 0)
    return pl.pallas_call(
        _ffn_kernel,
        grid=(b, l // rows),
        in_specs=[
            pl.BlockSpec((1, rows, D_MODEL), blk),
            pl.BlockSpec((1, rows, D_MODEL), blk),
            pl.BlockSpec((D_MODEL, 2 * D_FF), const, pipeline_mode=pl.Buffered(1)),
            pl.BlockSpec((FFN_CONV_WIDTH, 2 * D_FF), const),
            pl.BlockSpec((1, 2 * D_FF), const),
            pl.BlockSpec((D_FF, D_MODEL), const, pipeline_mode=pl.Buffered(1)),
        ],
        out_specs=pl.BlockSpec((1, rows, D_MODEL), blk),
        out_shape=jax.ShapeDtypeStruct((b, l, D_MODEL), F32),
        scratch_shapes=[
            pltpu.VMEM((SUBLANES, 2 * D_FF), F32),
            pltpu.VMEM((rows, D_FF), BF16),
        ],
        compiler_params=pltpu.CompilerParams(dimension_semantics=("arbitrary", "arbitrary"),
                                             vmem_limit_bytes=VMEM_LIMIT_BYTES),
        name="ffn",
    )(xn, x1, wup, cw, cb, wdn)


def _block_diag_pairs(w):
    z = jnp.zeros((RNN_BLOCK_DIM, RNN_BLOCK_DIM), w.dtype)
    w = w.reshape(RNN_BLOCKS // 2, 2, RNN_BLOCK_DIM, RNN_BLOCK_DIM)
    top = jnp.concatenate([w[:, 0], jnp.broadcast_to(z, w[:, 0].shape)], axis=-1)
    bot = jnp.concatenate([jnp.broadcast_to(z, w[:, 1].shape), w[:, 1]], axis=-1)
    return jnp.concatenate([top, bot], axis=-2)


def _layer(x, top_k, norm1_g, w_in, q_norm_g, k_norm_g, kidx_norm_g, conv_w, conv_b, rg_wa, rg_ba, rg_wx, rg_bx,
           rg_lambda, w_o_attn, w_o_rnn, w_out, norm2_g, w_up, ffn_conv_w, ffn_conv_b, w_down):
    b, l, _ = x.shape
    m = b * l
    x2 = x.reshape(m, D_MODEL)

    n_head = ATTN_WIDTH + 2 * KV_WIDTH + IDX_HEADS * IDX_DIM + IDX_DIM + IDX_HEADS
    pad = jnp.zeros((D_MODEL, ATTN_SLAB - n_head), w_in.dtype)
    w_all = jnp.concatenate([w_in[:, :n_head], pad, w_in[:, n_head:]], axis=1).astype(BF16)
    pair = lambda g: jnp.tile(g, 2).reshape(1, LANES)
    kig = jnp.concatenate([kidx_norm_g, jnp.zeros((LANES - IDX_DIM,), F32)]).reshape(1, LANES)

    qt, qit, wt, kib, k, vta, rx, rgate, ga, gb = _proj(x2, norm1_g.reshape(1, D_MODEL), w_all,
                                                          pair(q_norm_g), pair(k_norm_g), kig, b, l)
    attn = _dsa(qt, qit, wt, kib, k, vta, top_k)

    wg = jnp.concatenate([_block_diag_pairs(rg_wa), _block_diag_pairs(rg_wx)], axis=-1).astype(BF16)
    rnn = _rglru(rx.reshape(b, l, RNN_WIDTH), rgate.reshape(b, l, RNN_WIDTH), conv_w, conv_b.reshape(1, -1), wg,
                 rg_ba.reshape(1, -1), rg_bx.reshape(1, -1), rg_lambda.reshape(1, -1))

    x1, xn2 = _merge(attn.reshape(m, ATTN_WIDTH), rnn.reshape(m, RNN_WIDTH), ga, gb, x2,
                     w_o_attn.astype(BF16), w_o_rnn.astype(BF16), w_out.astype(BF16), norm2_g.reshape(1, D_MODEL))

    out = _ffn(xn2.reshape(b, l, D_MODEL), x1.reshape(b, l, D_MODEL), w_up.astype(BF16), ffn_conv_w,
               ffn_conv_b.reshape(1, -1), w_down.astype(BF16))
    return out


def kernel(x, norm1_g, w_in, q_norm_g, k_norm_g, kidx_norm_g, conv_w, conv_b, rg_wa, rg_ba, rg_wx, rg_bx, rg_lambda,
           w_o_attn, w_o_rnn, w_out, norm2_g, w_up, ffn_conv_w, ffn_conv_b, w_down):
    l = x.shape[1]
    top_k = min(TOPK_MAX, l // 4)
    h = x
    for d in range(norm1_g.shape[0]):
        h = _layer(h, top_k, norm1_g[d], w_in[d], q_norm_g[d], k_norm_g[d], kidx_norm_g[d], conv_w[d], conv_b[d],
                   rg_wa[d], rg_ba[d], rg_wx[d], rg_bx[d], rg_lambda[d], w_o_attn[d], w_o_rnn[d], w_out[d],
                   norm2_g[d], w_up[d], ffn_conv_w[d], ffn_conv_b[d], w_down[d])
    return h
```

```python
import functools

import jax
import jax.numpy as jnp
import numpy as np
from jax import lax
from jax.experimental import pallas as pl
from jax.experimental.pallas import tpu as pltpu

F32 = jnp.float32
BF16 = jnp.bfloat16
I32 = jnp.int32

D_MODEL = 1024
HEAD_DIM = 64
N_Q_HEADS = 8
N_KV_HEADS = 2
Q_PER_KV = N_Q_HEADS // N_KV_HEADS
ATTN_WIDTH = N_Q_HEADS * HEAD_DIM
KV_WIDTH = N_KV_HEADS * HEAD_DIM
IDX_HEADS = 4
IDX_DIM = 64
TOPK_MAX = 256
RNN_WIDTH = D_MODEL
RNN_BLOCKS = 16
RNN_BLOCK_DIM = RNN_WIDTH // RNN_BLOCKS
CONV_WIDTH = 4
LRU_C = 8.0
D_FF = 2816
FFN_CONV_WIDTH = 3
EPS = 1e-6

LANES = 128
SUBLANES = 8
PACK = 16
VMEM_LIMIT_BYTES = 56 * 1024 * 1024

PROJ_ROWS = 512
ATTN_SLAB = ATTN_WIDTH + 2 * KV_WIDTH + IDX_HEADS * IDX_DIM + LANES
Q_BLOCK = 256
KEY_CHUNK = 512
RNN_ROWS = 256
MERGE_ROWS = 512
FFN_ROWS = 512
FFN_CHUNK = 256

ATTN_PIECES = 4
ONES_ROWS = 16
V_ROWS = HEAD_DIM + ONES_ROWS

INT_MIN = -2 ** 31
LOG2E = 1.4426950408889634
MASK_BIAS = -1e30
M_INIT = -3e38


def _sigmoid(x):
    return 1.0 / (1.0 + jnp.exp(-x))


def _gelu_tanh(x):
    c = np.sqrt(2.0 / np.pi).astype(np.float32)
    return x * (0.5 * (1.0 + jnp.tanh(c * (x + 0.044715 * (x * x * x)))))


def _pair_norm(slab, gain):
    lane = lax.broadcasted_iota(I32, slab.shape, 1)
    lo = lane < HEAD_DIM
    sq = slab * slab
    s_lo = jnp.sum(jnp.where(lo, sq, 0.0), axis=-1, keepdims=True)
    s_hi = jnp.sum(jnp.where(lo, 0.0, sq), axis=-1, keepdims=True)
    inv = jnp.where(lo, lax.rsqrt(s_lo * (1.0 / HEAD_DIM) + EPS), lax.rsqrt(s_hi * (1.0 / HEAD_DIM) + EPS))
    return slab * inv * gain


def _proj_kernel(x_ref, g1_ref, wh_ref, ww_ref, qg_ref, kg_ref, kig_ref,
                 qt_ref, qit_ref, wt_ref, kib_ref, k_ref, vta_ref, rx_ref, rgate_ref, ga_ref, gb_ref):
    x = x_ref[...]
    rows = x.shape[0]
    ms = jnp.mean(x * x, axis=-1, keepdims=True)
    xn = (x * lax.rsqrt(ms + EPS) * g1_ref[...]).astype(BF16)

    pa = jnp.dot(xn, wh_ref[...], preferred_element_type=F32)
    qg = qg_ref[...]
    kg = kg_ref[...]

    def put_heads(ref, first_head, slab):
        t = slab.T
        ref[0, first_head] = t[0:HEAD_DIM].astype(ref.dtype)
        ref[0, first_head + 1] = t[HEAD_DIM:2 * HEAD_DIM].astype(ref.dtype)

    scale = (HEAD_DIM ** -0.5) * LOG2E
    for j in range(ATTN_WIDTH // LANES):
        sl = slice(j * LANES, (j + 1) * LANES)
        put_heads(qt_ref, 2 * j, _pair_norm(pa[:, sl], qg) * scale)
    off = ATTN_WIDTH
    k_ref[0] = _pair_norm(pa[:, off:off + KV_WIDTH], kg).astype(BF16)
    off += KV_WIDTH
    put_heads(vta_ref.at[:, :, 0:HEAD_DIM, :], 0, pa[:, off:off + KV_WIDTH])
    ones = jnp.ones((ONES_ROWS, rows), BF16)
    for g in range(N_KV_HEADS):
        vta_ref[0, g, HEAD_DIM:HEAD_DIM + ONES_ROWS, :] = ones
    off += KV_WIDTH
    for j in range(IDX_HEADS * IDX_DIM // LANES):
        put_heads(qit_ref, 2 * j, pa[:, off + j * LANES:off + (j + 1) * LANES] * (IDX_DIM ** -0.5))
    off += IDX_HEADS * IDX_DIM
    tail = pa[:, off:off + LANES]
    lane = lax.broadcasted_iota(I32, tail.shape, 1)
    is_ki = lane < IDX_DIM
    s_ki = jnp.sum(jnp.where(is_ki, tail * tail, 0.0), axis=-1, keepdims=True)
    ki_n = tail * lax.rsqrt(s_ki * (1.0 / IDX_DIM) + EPS) * kig_ref[...]
    kiwi = jnp.where(is_ki, ki_n, tail * (IDX_HEADS ** -0.5))
    kib_ref[0] = kiwi.astype(BF16)
    wt_ref[0] = kiwi.T[IDX_DIM:IDX_DIM + SUBLANES]

    off = 0
    for ref in (rx_ref, rgate_ref, ga_ref, gb_ref):
        ref[...] = jnp.dot(xn, ww_ref[:, off:off + D_MODEL], preferred_element_type=F32).astype(BF16)
        off += D_MODEL


def _proj(x2, g1, w_head, w_wide, qg, kg, kig, b, l):
    m = x2.shape[0]
    rows = PROJ_ROWS
    tpb = l // rows
    const = lambda i: (0, 0)
    row = lambda i: (i, 0)
    head_major = lambda i: (i // tpb, 0, 0, i % tpb)
    seq_major = lambda i: (i // tpb, i % tpb, 0)
    wide = [(D_MODEL, BF16)] * 4
    return pl.pallas_call(
        _proj_kernel,
        grid=(m // rows,),
        in_specs=[
            pl.BlockSpec((rows, D_MODEL), row),
            pl.BlockSpec((1, D_MODEL), const),
            pl.BlockSpec((D_MODEL, ATTN_SLAB), const, pipeline_mode=pl.Buffered(1)),
            pl.BlockSpec((D_MODEL, 4 * D_MODEL), const, pipeline_mode=pl.Buffered(1)),
            pl.BlockSpec((1, LANES), const),
            pl.BlockSpec((1, LANES), const),
            pl.BlockSpec((1, LANES), const),
        ],
        out_specs=[
            pl.BlockSpec((1, N_Q_HEADS, HEAD_DIM, rows), head_major),
            pl.BlockSpec((1, IDX_HEADS, IDX_DIM, rows), head_major),
            pl.BlockSpec((1, SUBLANES, rows), lambda i: (i // tpb, 0, i % tpb)),
            pl.BlockSpec((1, rows, LANES), seq_major),
            pl.BlockSpec((1, rows, LANES), seq_major),
            pl.BlockSpec((1, N_KV_HEADS, V_ROWS, rows), head_major),
        ] + [pl.BlockSpec((rows, w), row) for w, _ in wide],
        out_shape=[
            jax.ShapeDtypeStruct((b, N_Q_HEADS, HEAD_DIM, l), BF16),
            jax.ShapeDtypeStruct((b, IDX_HEADS, IDX_DIM, l), BF16),
            jax.ShapeDtypeStruct((b, SUBLANES, l), F32),
            jax.ShapeDtypeStruct((b, l, LANES), BF16),
            jax.ShapeDtypeStruct((b, l, LANES), BF16),
            jax.ShapeDtypeStruct((b, N_KV_HEADS, V_ROWS, l), BF16),
        ] + [jax.ShapeDtypeStruct((m, w), dt) for w, dt in wide],
        compiler_params=pltpu.CompilerParams(dimension_semantics=("arbitrary",), vmem_limit_bytes=VMEM_LIMIT_BYTES),
        name="proj",
    )(x2, g1, w_head, w_wide, qg, kg, kig)


def _dsa_kernel(top_k, n_chunks, qt_ref, qit_ref, wt_ref, kib_ref, k_ref, vta_ref, o_ref,
                score_ref, coarse_ref, s_ref, p_ref, m_ref, acc_ref):
    blk = (n_chunks - 1) * (KEY_CHUNK // Q_BLOCK) + pl.program_id(1)
    packed = KEY_CHUNK // PACK
    wide = Q_PER_KV * Q_BLOCK
    chunk = lambda c: slice(c * KEY_CHUNK, (c + 1) * KEY_CHUNK)

    qit = jnp.concatenate([qit_ref[0, h] for h in range(IDX_HEADS)], axis=1)
    qit = jnp.concatenate([qit, jnp.zeros_like(qit)], axis=0)
    wt = wt_ref[0]

    for c in range(n_chunks):
        d = jnp.dot(kib_ref[0, chunk(c), :], qit, preferred_element_type=F32)
        s = jnp.zeros((KEY_CHUNK, Q_BLOCK), F32)
        for h in range(IDX_HEADS):
            s = s + wt[h:h + 1, :] * jnp.maximum(d[:, h * Q_BLOCK:(h + 1) * Q_BLOCK], 0.0)
        if c == n_chunks - 1:
            q_pos = blk * Q_BLOCK + lax.broadcasted_iota(I32, (KEY_CHUNK, Q_BLOCK), 1)
            key_pos = c * KEY_CHUNK + lax.broadcasted_iota(I32, (KEY_CHUNK, Q_BLOCK), 0)
            s = jnp.where(key_pos <= q_pos, s, -jnp.inf)
        score_ref[chunk(c), :] = s
        coarse_ref[c * packed:(c + 1) * packed] = s.astype(BF16).reshape(packed, PACK, Q_BLOCK)

    n_acc = 4

    def order_flip(key):
        return key ^ ((key >> 31) & 0x7FFFFFFF)

    def as_float(key):
        return lax.bitcast_convert_type(order_flip(key), F32)

    one = jnp.ones((PACK, Q_BLOCK), BF16)
    zero = jnp.zeros((PACK, Q_BLOCK), BF16)

    def count_coarse(level):
        lvl = jnp.broadcast_to(level.astype(BF16), (PACK, Q_BLOCK))
        accs = [zero] * n_acc
        for i in range(n_chunks * packed):
            accs[i % n_acc] = accs[i % n_acc] + jnp.where(coarse_ref[i] >= lvl, one, zero)
        tot = (accs[0].astype(F32) + accs[1].astype(F32)) + (accs[2].astype(F32) + accs[3].astype(F32))
        return jnp.sum(tot, axis=0, keepdims=True)

    def count(compare, level):
        lvl = jnp.broadcast_to(level, (SUBLANES, Q_BLOCK))
        accs = [jnp.zeros((SUBLANES, Q_BLOCK), I32)] * n_acc
        for i in range(n_chunks * KEY_CHUNK // SUBLANES):
            hit = compare(score_ref[i * SUBLANES:(i + 1) * SUBLANES, :], lvl)
            accs[i % n_acc] = accs[i % n_acc] + jnp.where(hit, 1, 0)
        return jnp.sum((accs[0] + accs[1]) + (accs[2] + accs[3]), axis=0, keepdims=True)

    def coarse_body(it, cand):
        trial = cand + jnp.left_shift(jnp.int32(1), 31 - it)
        level = lax.bitcast_convert_type(order_flip(trial) & -65536, F32)
        return jnp.where(count_coarse(level) >= top_k, trial, cand)

    coarse = lax.fori_loop(0, 16, coarse_body, jnp.full((1, Q_BLOCK), INT_MIN, I32))
    centre = order_flip(order_flip(coarse) & -65536)
    base = centre - 32768

    def fine_body(it, offset):
        trial = offset + jnp.left_shift(jnp.int32(1), 16 - it)
        return jnp.where(count(jnp.greater_equal, as_float(base + trial)) >= top_k, trial, offset)

    thr = as_float(base + lax.fori_loop(0, 17, fine_body, jnp.zeros((1, Q_BLOCK), I32)))
    thr = jnp.where(coarse == INT_MIN, -jnp.inf, thr)
    need = jnp.where(thr == -jnp.inf, 0, top_k - count(jnp.greater, thr)).astype(F32)

    zeros_q = jnp.zeros((HEAD_DIM, wide), BF16)
    qt_groups = []
    for g in range(N_KV_HEADS):
        qg = jnp.concatenate([qt_ref[0, g * Q_PER_KV + r] for r in range(Q_PER_KV)], axis=1)
        qt_groups.append(jnp.concatenate([qg, zeros_q] if g == 0 else [zeros_q, qg], axis=0))
    half = KEY_CHUNK // 2
    tri_r = lax.broadcasted_iota(I32, (KEY_CHUNK, KEY_CHUNK), 0)
    tri_c = lax.broadcasted_iota(I32, (KEY_CHUNK, KEY_CHUNK), 1)
    prefix_mat = jnp.where(tri_c <= tri_r, 1.0, 0.0).astype(BF16)

    m_ref[...] = jnp.full(m_ref.shape, M_INIT, F32)
    acc_ref[...] = jnp.zeros(acc_ref.shape, F32)
    piece = KEY_CHUNK // ATTN_PIECES

    def mask_bias(c, ties_before):
        kc = score_ref[chunk(c), :]
        is_tie = kc == thr
        tie = jnp.where(is_tie, 1.0, 0.0).astype(BF16)
        rank_top = jnp.dot(prefix_mat[0:half, 0:half], tie[0:half], preferred_element_type=F32)
        rank_bot = jnp.dot(prefix_mat[half:KEY_CHUNK, :], tie, preferred_element_type=F32)
        tie_rank = ties_before + jnp.concatenate([rank_top, rank_bot], axis=0)
        admitted = jnp.where(is_tie, tie_rank, 2.0 * KEY_CHUNK * KEY_CHUNK) <= need
        bias = jnp.where(kc > thr, 0.0, jnp.where(admitted, 0.0, MASK_BIAS))
        return bias, tie_rank[KEY_CHUNK - 1:KEY_CHUNK, :]

    def raw_scores(c, g):
        s_ref[g] = jnp.dot(k_ref[0, chunk(c), :], qt_groups[g], preferred_element_type=F32)

    def softmax_pv(c, g, bias):
        mx = None
        for j in range(ATTN_PIECES):
            rows = slice(j * piece, (j + 1) * piece)
            s = s_ref[g, rows, :] + jnp.concatenate([bias[rows]] * Q_PER_KV, axis=1)
            s_ref[g, rows, :] = s
            pm = jnp.max(s.reshape(piece // SUBLANES, SUBLANES, wide), axis=0)
            mx = pm if mx is None else jnp.maximum(mx, pm)
        m_old = m_ref[g]
        m_new = jnp.maximum(m_old, jnp.max(mx, axis=0, keepdims=True))
        m_ref[g] = m_new
        for j in range(ATTN_PIECES):
            rows = slice(j * piece, (j + 1) * piece)
            p_ref[g, rows, :] = jnp.exp2(s_ref[g, rows, :] - m_new).astype(BF16)
        pv = jnp.dot(vta_ref[0, g, :, chunk(c)], p_ref[g], preferred_element_type=F32)
        acc_ref[g] = jnp.exp2(m_old - m_new) * acc_ref[g] + pv

    bias, ties = mask_bias(0, jnp.zeros((1, Q_BLOCK), F32))
    raw_scores(0, 0)
    for c in range(n_chunks):
        raw_scores(c, 1)
        softmax_pv(c, 0, bias)
        if c + 1 < n_chunks:
            bias_next, ties = mask_bias(c + 1, ties)
            raw_scores(c + 1, 0)
        softmax_pv(c, 1, bias)
        if c + 1 < n_chunks:
            bias = bias_next

    for g in range(N_KV_HEADS):
        acc = acc_ref[g]
        out = acc[0:HEAD_DIM, :] / acc[HEAD_DIM:HEAD_DIM + 1, :]
        for r in range(0, Q_PER_KV, 2):
            pair = jnp.concatenate([out[:, r * Q_BLOCK:(r + 1) * Q_BLOCK], out[:, (r + 1) * Q_BLOCK:(r + 2) * Q_BLOCK]],
                                   axis=0)
            col = (g * Q_PER_KV + r) * HEAD_DIM
            o_ref[0, :, col:col + 2 * HEAD_DIM] = pair.T.astype(o_ref.dtype)


def _dsa(qt, qit, wt, kib, k, vta, top_k):
    b, _, _, l = qt.shape
    assert l // PACK <= 256, "bf16 counters hold at most 256 keys per slot"
    wide = Q_PER_KV * Q_BLOCK
    per = KEY_CHUNK // Q_BLOCK
    outs = []
    for n_chunks in range(1, l // KEY_CHUNK + 1):
        first = (n_chunks - 1) * per
        keys = n_chunks * KEY_CHUNK
        outs.append(pl.pallas_call(
            functools.partial(_dsa_kernel, top_k, n_chunks),
            grid=(b, per),
            in_specs=[
                pl.BlockSpec((1, N_Q_HEADS, HEAD_DIM, Q_BLOCK), lambda bi, i, first=first: (bi, 0, 0, first + i)),
                pl.BlockSpec((1, IDX_HEADS, IDX_DIM, Q_BLOCK), lambda bi, i, first=first: (bi, 0, 0, first + i)),
                pl.BlockSpec((1, SUBLANES, Q_BLOCK), lambda bi, i, first=first: (bi, 0, first + i)),
                pl.BlockSpec((1, keys, LANES), lambda bi, i: (bi, 0, 0)),
                pl.BlockSpec((1, keys, LANES), lambda bi, i: (bi, 0, 0)),
                pl.BlockSpec((1, N_KV_HEADS, V_ROWS, keys), lambda bi, i: (bi, 0, 0, 0)),
            ],
            out_specs=pl.BlockSpec((1, Q_BLOCK, ATTN_WIDTH), lambda bi, i: (bi, i, 0)),
            out_shape=jax.ShapeDtypeStruct((b, KEY_CHUNK, ATTN_WIDTH), BF16),
            scratch_shapes=[
                pltpu.VMEM((keys, Q_BLOCK), F32),
                pltpu.VMEM((keys // PACK, PACK, Q_BLOCK), BF16),
                pltpu.VMEM((N_KV_HEADS, KEY_CHUNK, wide), F32),
                pltpu.VMEM((N_KV_HEADS, KEY_CHUNK, wide), BF16),
                pltpu.VMEM((N_KV_HEADS, 1, wide), F32),
                pltpu.VMEM((N_KV_HEADS, V_ROWS, wide), F32),
            ],
            compiler_params=pltpu.CompilerParams(dimension_semantics=("arbitrary", "arbitrary"),
                                                 vmem_limit_bytes=VMEM_LIMIT_BYTES),
            name=f"dsa{n_chunks}",
        )(qt, qit, wt, kib, k, vta))
    return jnp.concatenate(outs, axis=1)


def _rglru_kernel(rx_ref, rgate_ref, cw_ref, cb_ref, wg_ref, ba_ref, bx_ref, lam_ref, o_ref, xbuf_ref, h_ref):
    rows = rx_ref.shape[1]

    @pl.when(pl.program_id(1) == 0)
    def _():
        xbuf_ref[...] = jnp.zeros(xbuf_ref.shape, F32)
        h_ref[...] = jnp.zeros(h_ref.shape, F32)

    x = rx_ref[0].astype(F32)
    groups = rows // SUBLANES
    sub = lax.broadcasted_iota(I32, (groups, SUBLANES, LANES), 1)
    sub1 = sub[0]

    for n in range(RNN_WIDTH // LANES):
        sl = slice(n * LANES, (n + 1) * LANES)
        xs = x[:, sl]
        cw = cw_ref[:, sl]
        cb = cb_ref[:, sl]
        taps = lambda x3, x2, x1, x0: cb + cw[0:1] * x3 + cw[1:2] * x2 + cw[2:3] * x1 + cw[3:4] * x0
        y = taps(pltpu.roll(xs, 3, axis=0), pltpu.roll(xs, 2, axis=0), pltpu.roll(xs, 1, axis=0), xs)
        tail = xbuf_ref[:, sl]
        head = xs[0:SUBLANES]
        back = lambda k: jnp.where(sub1 < k, pltpu.roll(tail, k, axis=0), pltpu.roll(head, k, axis=0))
        xc = jnp.concatenate([taps(back(3), back(2), back(1), head), y[SUBLANES:]], axis=0)
        gates = jnp.dot(xc.astype(BF16), wg_ref[n], preferred_element_type=F32)
        r = _sigmoid(gates[:, 0:LANES] + ba_ref[:, sl])
        i = _sigmoid(gates[:, LANES:2 * LANES] + bx_ref[:, sl])
        lam = lam_ref[:, sl]
        log_sig = jnp.minimum(lam, 0.0) - jnp.log1p(jnp.exp(-jnp.abs(lam)))
        log_a = LRU_C * r * log_sig
        a = jnp.exp(log_a)
        b = jnp.sqrt(1.0 - a * a) * (i * xc)
        a3 = a.reshape(groups, SUBLANES, LANES)
        b3 = b.reshape(groups, SUBLANES, LANES)
        d = 1
        while d < SUBLANES:
            live = sub >= d
            a_prev = jnp.where(live, pltpu.roll(a3, d, axis=1), 1.0)
            b_prev = jnp.where(live, pltpu.roll(b3, d, axis=1), 0.0)
            b3 = a3 * b_prev + b3
            a3 = a3 * a_prev
            d *= 2
        h_prev = h_ref[:, sl]
        hs = []
        for t in range(groups):
            h_t = a3[t] * h_prev + b3[t]
            hs.append(h_t)
            h_prev = h_t[SUBLANES - 1:SUBLANES, :]
        h_ref[:, sl] = h_prev
        h = jnp.concatenate(hs, axis=0)
        o_ref[0, :, sl] = (h * _gelu_tanh(rgate_ref[0, :, sl].astype(F32))).astype(o_ref.dtype)

    xbuf_ref[...] = x[rows - SUBLANES:rows, :]


def _rglru(rx, rgate, cw, cb, wg, ba, bx, lam):
    b, l, _ = rx.shape
    rows = RNN_ROWS
    blk = lambda bi, i: (bi, i, 0)
    c2 = lambda bi, i: (0, 0)
    c3 = lambda bi, i: (0, 0, 0)
    return pl.pallas_call(
        _rglru_kernel,
        grid=(b, l // rows),
        in_specs=[
            pl.BlockSpec((1, rows, RNN_WIDTH), blk),
            pl.BlockSpec((1, rows, RNN_WIDTH), blk),
            pl.BlockSpec((CONV_WIDTH, RNN_WIDTH), c2),
            pl.BlockSpec((1, RNN_WIDTH), c2),
            pl.BlockSpec((RNN_WIDTH // LANES, LANES, 2 * LANES), c3),
            pl.BlockSpec((1, RNN_WIDTH), c2),
            pl.BlockSpec((1, RNN_WIDTH), c2),
            pl.BlockSpec((1, RNN_WIDTH), c2),
        ],
        out_specs=pl.BlockSpec((1, rows, RNN_WIDTH), blk),
        out_shape=jax.ShapeDtypeStruct((b, l, RNN_WIDTH), BF16),
        scratch_shapes=[
            pltpu.VMEM((SUBLANES, RNN_WIDTH), F32),
            pltpu.VMEM((1, RNN_WIDTH), F32),
        ],
        compiler_params=pltpu.CompilerParams(dimension_semantics=("arbitrary", "arbitrary"),
                                             vmem_limit_bytes=VMEM_LIMIT_BYTES),
        name="rglru",
    )(rx, rgate, cw, cb, wg, ba, bx, lam)


def _merge_kernel(attn_ref, rnn_ref, ga_ref, gb_ref, x_ref, woa_ref, wor_ref, wout_ref, g2_ref, x1_ref, xn_ref):
    a = jnp.dot(attn_ref[...], woa_ref[...], preferred_element_type=F32)
    r = jnp.dot(rnn_ref[...], wor_ref[...], preferred_element_type=F32)
    merged = _sigmoid(ga_ref[...].astype(F32)) * a + _sigmoid(gb_ref[...].astype(F32)) * r
    x1 = x_ref[...] + jnp.dot(merged.astype(BF16), wout_ref[...], preferred_element_type=F32)
    x1_ref[...] = x1
    ms = jnp.mean(x1 * x1, axis=-1, keepdims=True)
    xn_ref[...] = (x1 * lax.rsqrt(ms + EPS) * g2_ref[...]).astype(BF16)


def _merge(attn, rnn, ga, gb, x2, woa, wor, wout, g2):
    m = x2.shape[0]
    rows = MERGE_ROWS
    row = lambda i: (i, 0)
    const = lambda i: (0, 0)
    return pl.pallas_call(
        _merge_kernel,
        grid=(m // rows,),
        in_specs=[
            pl.BlockSpec((rows, ATTN_WIDTH), row),
            pl.BlockSpec((rows, RNN_WIDTH), row),
            pl.BlockSpec((rows, D_MODEL), row),
            pl.BlockSpec((rows, D_MODEL), row),
            pl.BlockSpec((rows, D_MODEL), row),
            pl.BlockSpec((ATTN_WIDTH, D_MODEL), const, pipeline_mode=pl.Buffered(1)),
            pl.BlockSpec((RNN_WIDTH, D_MODEL), const, pipeline_mode=pl.Buffered(1)),
            pl.BlockSpec((D_MODEL, D_MODEL), const, pipeline_mode=pl.Buffered(1)),
            pl.BlockSpec((1, D_MODEL), const),
        ],
        out_specs=[pl.BlockSpec((rows, D_MODEL), row), pl.BlockSpec((rows, D_MODEL), row)],
        out_shape=[jax.ShapeDtypeStruct((m, D_MODEL), F32), jax.ShapeDtypeStruct((m, D_MODEL), BF16)],
        compiler_params=pltpu.CompilerParams(dimension_semantics=("arbitrary",), vmem_limit_bytes=VMEM_LIMIT_BYTES),
        name="merge",
    )(attn, rnn, ga, gb, x2, woa, wor, wout, g2)


def _ffn_kernel(xn_ref, x1_ref, wup_ref, cw_ref, cb_ref, wdn_ref, o_ref, tail_ref, hid_ref):
    rows = xn_ref.shape[1]

    @pl.when(pl.program_id(1) == 0)
    def _():
        tail_ref[...] = jnp.zeros(tail_ref.shape, F32)

    xn = xn_ref[0]
    sub = lax.broadcasted_iota(I32, (SUBLANES, FFN_CHUNK), 0)

    def conv(col0):
        cols = slice(col0, col0 + FFN_CHUNK)
        u = jnp.dot(xn, wup_ref[:, cols], preferred_element_type=F32)
        cw = cw_ref[:, cols]
        cb = cb_ref[:, cols]
        taps = lambda u2, u1, u0: cb + cw[0:1] * u2 + cw[1:2] * u1 + cw[2:3] * u0
        y = taps(pltpu.roll(u, 2, axis=0), pltpu.roll(u, 1, axis=0), u)
        tail = tail_ref[:, cols]
        head = u[0:SUBLANES]
        prev1 = tail[SUBLANES - 1:SUBLANES]
        prev2 = tail[SUBLANES - 2:SUBLANES - 1]
        h1 = jnp.where(sub == 0, prev1, pltpu.roll(head, 1, axis=0))
        h2 = jnp.where(sub == 0, prev2, jnp.where(sub == 1, prev1, pltpu.roll(head, 2, axis=0)))
        tail_ref[:, cols] = u[rows - SUBLANES:rows]
        return jnp.concatenate([taps(h2, h1, head), y[SUBLANES:]], axis=0)

    for j in range(D_FF // FFN_CHUNK):
        gate = conv(j * FFN_CHUNK)
        val = conv(D_FF + j * FFN_CHUNK)
        hid_ref[:, j * FFN_CHUNK:(j + 1) * FFN_CHUNK] = (gate * _sigmoid(gate) * val).astype(BF16)
    o_ref[0] = x1_ref[0] + jnp.dot(hid_ref[...], wdn_ref[...], preferred_element_type=F32)


def _ffn(xn, x1, wup, cw, cb, wdn):
    b, l, _ = xn.shape
    rows = FFN_ROWS
    blk = lambda bi, i: (bi, i, 0)
    const = lambda bi, i: (0, 0)
    return pl.pallas_call(
        _ffn_kernel,
        grid=(b, l // rows),
        in_specs=[
            pl.BlockSpec((1, rows, D_MODEL), blk),
            pl.BlockSpec((1, rows, D_MODEL), blk),
            pl.BlockSpec((D_MODEL, 2 * D_FF), const, pipeline_mode=pl.Buffered(1)),
            pl.BlockSpec((FFN_CONV_WIDTH, 2 * D_FF), const),
            pl.BlockSpec((1, 2 * D_FF), const),
            pl.BlockSpec((D_FF, D_MODEL), const, pipeline_mode=pl.Buffered(1)),
        ],
        out_specs=pl.BlockSpec((1, rows, D_MODEL), blk),
        out_shape=jax.ShapeDtypeStruct((b, l, D_MODEL), F32),
        scratch_shapes=[
            pltpu.VMEM((SUBLANES, 2 * D_FF), F32),
            pltpu.VMEM((rows, D_FF), BF16),
        ],
        compiler_params=pltpu.CompilerParams(dimension_semantics=("arbitrary", "arbitrary"),
                                             vmem_limit_bytes=VMEM_LIMIT_BYTES),
        name="ffn",
    )(xn, x1, wup, cw, cb, wdn)


def _block_diag_pairs(w):
    z = jnp.zeros((RNN_BLOCK_DIM, RNN_BLOCK_DIM), w.dtype)
    w = w.reshape(RNN_BLOCKS // 2, 2, RNN_BLOCK_DIM, RNN_BLOCK_DIM)
    top = jnp.concatenate([w[:, 0], jnp.broadcast_to(z, w[:, 0].shape)], axis=-1)
    bot = jnp.concatenate([jnp.broadcast_to(z, w[:, 1].shape), w[:, 1]], axis=-1)
    return jnp.concatenate([top, bot], axis=-2)


def _layer(x, top_k, norm1_g, w_in, q_norm_g, k_norm_g, kidx_norm_g, conv_w, conv_b, rg_wa, rg_ba, rg_wx, rg_bx,
           rg_lambda, w_o_attn, w_o_rnn, w_out, norm2_g, w_up, ffn_conv_w, ffn_conv_b, w_down):
    b, l, _ = x.shape
    m = b * l
    x2 = x.reshape(m, D_MODEL)

    n_head = ATTN_WIDTH + 2 * KV_WIDTH + IDX_HEADS * IDX_DIM + IDX_DIM + IDX_HEADS
    pad = jnp.zeros((D_MODEL, ATTN_SLAB - n_head), w_in.dtype)
    w_head = jnp.concatenate([w_in[:, :n_head], pad], axis=1).astype(BF16)
    w_wide = w_in[:, n_head:].astype(BF16)
    pair = lambda g: jnp.tile(g, 2).reshape(1, LANES)
    kig = jnp.concatenate([kidx_norm_g, jnp.zeros((LANES - IDX_DIM,), F32)]).reshape(1, LANES)

    qt, qit, wt, kib, k, vta, rx, rgate, ga, gb = _proj(x2, norm1_g.reshape(1, D_MODEL), w_head, w_wide,
                                                          pair(q_norm_g), pair(k_norm_g), kig, b, l)
    attn = _dsa(qt, qit, wt, kib, k, vta, top_k)

    wg = jnp.concatenate([_block_diag_pairs(rg_wa), _block_diag_pairs(rg_wx)], axis=-1).astype(BF16)
    rnn = _rglru(rx.reshape(b, l, RNN_WIDTH), rgate.reshape(b, l, RNN_WIDTH), conv_w, conv_b.reshape(1, -1), wg,
                 rg_ba.reshape(1, -1), rg_bx.reshape(1, -1), rg_lambda.reshape(1, -1))

    x1, xn2 = _merge(attn.reshape(m, ATTN_WIDTH), rnn.reshape(m, RNN_WIDTH), ga, gb, x2,
                     w_o_attn.astype(BF16), w_o_rnn.astype(BF16), w_out.astype(BF16), norm2_g.reshape(1, D_MODEL))

    out = _ffn(xn2.reshape(b, l, D_MODEL), x1.reshape(b, l, D_MODEL), w_up.astype(BF16), ffn_conv_w,
               ffn_conv_b.reshape(1, -1), w_down.astype(BF16))
    return out


def kernel(x, norm1_g, w_in, q_norm_g, k_norm_g, kidx_norm_g, conv_w, conv_b, rg_wa, rg_ba, rg_wx, rg_bx, rg_lambda,
           w_o_attn, w_o_rnn, w_out, norm2_g, w_up, ffn_conv_w, ffn_conv_b, w_down):
    l = x.shape[1]
    top_k = min(TOPK_MAX, l // 4)
    h = x
    for d in range(norm1_g.shape[0]):
        h = _layer(h, top_k, norm1_g[d], w_in[d], q_norm_g[d], k_norm_g[d], kidx_norm_g[d], conv_w[d], conv_b[d],
                   rg_wa[d], rg_ba[d], rg_wx[d], rg_bx[d], rg_lambda[d], w_o_attn[d], w_o_rnn[d], w_out[d],
                   norm2_g[d], w_up[d], ffn_conv_w[d], ffn_conv_b[d], w_down[d])
    return h
```

```python
import functools

import jax
import jax.numpy as jnp
import numpy as np
from jax import lax
from jax.experimental import pallas as pl
from jax.experimental.pallas import tpu as pltpu

F32 = jnp.float32
BF16 = jnp.bfloat16
I32 = jnp.int32

D_MODEL = 1024
HEAD_DIM = 64
N_Q_HEADS = 8
N_KV_HEADS = 2
Q_PER_KV = N_Q_HEADS // N_KV_HEADS
ATTN_WIDTH = N_Q_HEADS * HEAD_DIM
KV_WIDTH = N_KV_HEADS * HEAD_DIM
IDX_HEADS = 4
IDX_DIM = 64
TOPK_MAX = 256
RNN_WIDTH = D_MODEL
RNN_BLOCKS = 16
RNN_BLOCK_DIM = RNN_WIDTH // RNN_BLOCKS
CONV_WIDTH = 4
LRU_C = 8.0
D_FF = 2816
FFN_CONV_WIDTH = 3
EPS = 1e-6

LANES = 128
SUBLANES = 8
PACK = 16
VMEM_LIMIT_BYTES = 56 * 1024 * 1024

PROJ_ROWS = 512
ATTN_SLAB = ATTN_WIDTH + 2 * KV_WIDTH + IDX_HEADS * IDX_DIM + LANES
Q_BLOCK = 256
KEY_CHUNK = 256
RNN_ROWS = 256
MERGE_ROWS = 512
FFN_ROWS = 512
FFN_CHUNK = 256

ATTN_PIECES = 4
ONES_ROWS = 16
V_ROWS = HEAD_DIM + ONES_ROWS

INT_MIN = -2 ** 31
LOG2E = 1.4426950408889634
MASK_BIAS = -1e30
M_INIT = -3e38


def _sigmoid(x):
    return 1.0 / (1.0 + jnp.exp(-x))


def _gelu_tanh(x):
    c = np.sqrt(2.0 / np.pi).astype(np.float32)
    return x * (0.5 * (1.0 + jnp.tanh(c * (x + 0.044715 * (x * x * x)))))


def _pair_norm(slab, gain):
    lane = lax.broadcasted_iota(I32, slab.shape, 1)
    lo = lane < HEAD_DIM
    sq = slab * slab
    s_lo = jnp.sum(jnp.where(lo, sq, 0.0), axis=-1, keepdims=True)
    s_hi = jnp.sum(jnp.where(lo, 0.0, sq), axis=-1, keepdims=True)
    inv = jnp.where(lo, lax.rsqrt(s_lo * (1.0 / HEAD_DIM) + EPS), lax.rsqrt(s_hi * (1.0 / HEAD_DIM) + EPS))
    return slab * inv * gain


def _proj_kernel(x_ref, g1_ref, wh_ref, ww_ref, qg_ref, kg_ref, kig_ref,
                 qt_ref, qit_ref, wt_ref, kib_ref, k_ref, vta_ref, rx_ref, rgate_ref, ga_ref, gb_ref):
    x = x_ref[...]
    rows = x.shape[0]
    ms = jnp.mean(x * x, axis=-1, keepdims=True)
    xn = (x * lax.rsqrt(ms + EPS) * g1_ref[...]).astype(BF16)

    pa = jnp.dot(xn, wh_ref[...], preferred_element_type=F32)
    qg = qg_ref[...]
    kg = kg_ref[...]

    def put_heads(ref, first_head, slab):
        t = slab.T
        ref[0, first_head] = t[0:HEAD_DIM].astype(ref.dtype)
        ref[0, first_head + 1] = t[HEAD_DIM:2 * HEAD_DIM].astype(ref.dtype)

    scale = (HEAD_DIM ** -0.5) * LOG2E
    for j in range(ATTN_WIDTH // LANES):
        sl = slice(j * LANES, (j + 1) * LANES)
        put_heads(qt_ref, 2 * j, _pair_norm(pa[:, sl], qg) * scale)
    off = ATTN_WIDTH
    k_ref[0] = _pair_norm(pa[:, off:off + KV_WIDTH], kg).astype(BF16)
    off += KV_WIDTH
    put_heads(vta_ref.at[:, :, 0:HEAD_DIM, :], 0, pa[:, off:off + KV_WIDTH])
    ones = jnp.ones((ONES_ROWS, rows), BF16)
    for g in range(N_KV_HEADS):
        vta_ref[0, g, HEAD_DIM:HEAD_DIM + ONES_ROWS, :] = ones
    off += KV_WIDTH
    for j in range(IDX_HEADS * IDX_DIM // LANES):
        put_heads(qit_ref, 2 * j, pa[:, off + j * LANES:off + (j + 1) * LANES] * (IDX_DIM ** -0.5))
    off += IDX_HEADS * IDX_DIM
    tail = pa[:, off:off + LANES]
    lane = lax.broadcasted_iota(I32, tail.shape, 1)
    is_ki = lane < IDX_DIM
    s_ki = jnp.sum(jnp.where(is_ki, tail * tail, 0.0), axis=-1, keepdims=True)
    ki_n = tail * lax.rsqrt(s_ki * (1.0 / IDX_DIM) + EPS) * kig_ref[...]
    kiwi = jnp.where(is_ki, ki_n, tail * (IDX_HEADS ** -0.5))
    kib_ref[0] = kiwi.astype(BF16)
    wt_ref[0] = kiwi.T[IDX_DIM:IDX_DIM + SUBLANES]

    off = 0
    for ref in (rx_ref, rgate_ref, ga_ref, gb_ref):
        ref[...] = jnp.dot(xn, ww_ref[:, off:off + D_MODEL], preferred_element_type=F32).astype(BF16)
        off += D_MODEL


def _proj(x2, g1, w_head, w_wide, qg, kg, kig, b, l):
    m = x2.shape[0]
    rows = PROJ_ROWS
    tpb = l // rows
    const = lambda i: (0, 0)
    row = lambda i: (i, 0)
    head_major = lambda i: (i // tpb, 0, 0, i % tpb)
    seq_major = lambda i: (i // tpb, i % tpb, 0)
    wide = [(D_MODEL, BF16)] * 4
    return pl.pallas_call(
        _proj_kernel,
        grid=(m // rows,),
        in_specs=[
            pl.BlockSpec((rows, D_MODEL), row),
            pl.BlockSpec((1, D_MODEL), const),
            pl.BlockSpec((D_MODEL, ATTN_SLAB), const, pipeline_mode=pl.Buffered(1)),
            pl.BlockSpec((D_MODEL, 4 * D_MODEL), const, pipeline_mode=pl.Buffered(1)),
            pl.BlockSpec((1, LANES), const),
            pl.BlockSpec((1, LANES), const),
            pl.BlockSpec((1, LANES), const),
        ],
        out_specs=[
            pl.BlockSpec((1, N_Q_HEADS, HEAD_DIM, rows), head_major),
            pl.BlockSpec((1, IDX_HEADS, IDX_DIM, rows), head_major),
            pl.BlockSpec((1, SUBLANES, rows), lambda i: (i // tpb, 0, i % tpb)),
            pl.BlockSpec((1, rows, LANES), seq_major),
            pl.BlockSpec((1, rows, LANES), seq_major),
            pl.BlockSpec((1, N_KV_HEADS, V_ROWS, rows), head_major),
        ] + [pl.BlockSpec((rows, w), row) for w, _ in wide],
        out_shape=[
            jax.ShapeDtypeStruct((b, N_Q_HEADS, HEAD_DIM, l), BF16),
            jax.ShapeDtypeStruct((b, IDX_HEADS, IDX_DIM, l), BF16),
            jax.ShapeDtypeStruct((b, SUBLANES, l), F32),
            jax.ShapeDtypeStruct((b, l, LANES), BF16),
            jax.ShapeDtypeStruct((b, l, LANES), BF16),
            jax.ShapeDtypeStruct((b, N_KV_HEADS, V_ROWS, l), BF16),
        ] + [jax.ShapeDtypeStruct((m, w), dt) for w, dt in wide],
        compiler_params=pltpu.CompilerParams(dimension_semantics=("arbitrary",), vmem_limit_bytes=VMEM_LIMIT_BYTES),
        name="proj",
    )(x2, g1, w_head, w_wide, qg, kg, kig)


def _dsa_kernel(top_k, n_chunks, qt_ref, qit_ref, wt_ref, kib_ref, k_ref, vta_ref, o_ref,
                score_ref, coarse_ref, s_ref, p_ref, m_ref, acc_ref):
    blk = (n_chunks - 1) * (KEY_CHUNK // Q_BLOCK) + pl.program_id(1)
    packed = KEY_CHUNK // PACK
    wide = Q_PER_KV * Q_BLOCK
    chunk = lambda c: slice(c * KEY_CHUNK, (c + 1) * KEY_CHUNK)

    qit = jnp.concatenate([qit_ref[0, h] for h in range(IDX_HEADS)], axis=1)
    qit = jnp.concatenate([qit, jnp.zeros_like(qit)], axis=0)
    wt = wt_ref[0]

    for c in range(n_chunks):
        d = jnp.dot(kib_ref[0, chunk(c), :], qit, preferred_element_type=F32)
        s = jnp.zeros((KEY_CHUNK, Q_BLOCK), F32)
        for h in range(IDX_HEADS):
            s = s + wt[h:h + 1, :] * jnp.maximum(d[:, h * Q_BLOCK:(h + 1) * Q_BLOCK], 0.0)
        if c == n_chunks - 1:
            q_pos = blk * Q_BLOCK + lax.broadcasted_iota(I32, (KEY_CHUNK, Q_BLOCK), 1)
            key_pos = c * KEY_CHUNK + lax.broadcasted_iota(I32, (KEY_CHUNK, Q_BLOCK), 0)
            s = jnp.where(key_pos <= q_pos, s, -jnp.inf)
        score_ref[chunk(c), :] = s
        coarse_ref[c * packed:(c + 1) * packed] = s.astype(BF16).reshape(packed, PACK, Q_BLOCK)

    n_acc = 4

    def order_flip(key):
        return key ^ ((key >> 31) & 0x7FFFFFFF)

    def as_float(key):
        return lax.bitcast_convert_type(order_flip(key), F32)

    one = jnp.ones((PACK, Q_BLOCK), BF16)
    zero = jnp.zeros((PACK, Q_BLOCK), BF16)

    def count_coarse(level):
        lvl = jnp.broadcast_to(level.astype(BF16), (PACK, Q_BLOCK))
        accs = [zero] * n_acc
        for i in range(n_chunks * packed):
            accs[i % n_acc] = accs[i % n_acc] + jnp.where(coarse_ref[i] >= lvl, one, zero)
        tot = (accs[0].astype(F32) + accs[1].astype(F32)) + (accs[2].astype(F32) + accs[3].astype(F32))
        return jnp.sum(tot, axis=0, keepdims=True)

    def count(compare, level):
        lvl = jnp.broadcast_to(level, (SUBLANES, Q_BLOCK))
        accs = [jnp.zeros((SUBLANES, Q_BLOCK), I32)] * n_acc
        for i in range(n_chunks * KEY_CHUNK // SUBLANES):
            hit = compare(score_ref[i * SUBLANES:(i + 1) * SUBLANES, :], lvl)
            accs[i % n_acc] = accs[i % n_acc] + jnp.where(hit, 1, 0)
        return jnp.sum((accs[0] + accs[1]) + (accs[2] + accs[3]), axis=0, keepdims=True)

    def coarse_body(it, cand):
        trial = cand + jnp.left_shift(jnp.int32(1), 31 - it)
        level = lax.bitcast_convert_type(order_flip(trial) & -65536, F32)
        return jnp.where(count_coarse(level) >= top_k, trial, cand)

    coarse = lax.fori_loop(0, 16, coarse_body, jnp.full((1, Q_BLOCK), INT_MIN, I32))
    centre = order_flip(order_flip(coarse) & -65536)
    base = centre - 32768

    def fine_body(it, offset):
        trial = offset + jnp.left_shift(jnp.int32(1), 16 - it)
        return jnp.where(count(jnp.greater_equal, as_float(base + trial)) >= top_k, trial, offset)

    thr = as_float(base + lax.fori_loop(0, 17, fine_body, jnp.zeros((1, Q_BLOCK), I32)))
    thr = jnp.where(coarse == INT_MIN, -jnp.inf, thr)
    need = jnp.where(thr == -jnp.inf, 0, top_k - count(jnp.greater, thr)).astype(F32)

    zeros_q = jnp.zeros((HEAD_DIM, wide), BF16)
    qt_groups = []
    for g in range(N_KV_HEADS):
        qg = jnp.concatenate([qt_ref[0, g * Q_PER_KV + r] for r in range(Q_PER_KV)], axis=1)
        qt_groups.append(jnp.concatenate([qg, zeros_q] if g == 0 else [zeros_q, qg], axis=0))
    half = KEY_CHUNK // 2
    tri_r = lax.broadcasted_iota(I32, (KEY_CHUNK, KEY_CHUNK), 0)
    tri_c = lax.broadcasted_iota(I32, (KEY_CHUNK, KEY_CHUNK), 1)
    prefix_mat = jnp.where(tri_c <= tri_r, 1.0, 0.0).astype(BF16)

    m_ref[...] = jnp.full(m_ref.shape, M_INIT, F32)
    acc_ref[...] = jnp.zeros(acc_ref.shape, F32)
    piece = KEY_CHUNK // ATTN_PIECES

    def mask_bias(c, ties_before):
        kc = score_ref[chunk(c), :]
        is_tie = kc == thr
        tie = jnp.where(is_tie, 1.0, 0.0).astype(BF16)
        rank_top = jnp.dot(prefix_mat[0:half, 0:half], tie[0:half], preferred_element_type=F32)
        rank_bot = jnp.dot(prefix_mat[half:KEY_CHUNK, :], tie, preferred_element_type=F32)
        tie_rank = ties_before + jnp.concatenate([rank_top, rank_bot], axis=0)
        admitted = jnp.where(is_tie, tie_rank, 2.0 * KEY_CHUNK * KEY_CHUNK) <= need
        bias = jnp.where(kc > thr, 0.0, jnp.where(admitted, 0.0, MASK_BIAS))
        return bias, tie_rank[KEY_CHUNK - 1:KEY_CHUNK, :]

    def raw_scores(c, g):
        s_ref[g] = jnp.dot(k_ref[0, chunk(c), :], qt_groups[g], preferred_element_type=F32)

    def softmax_pv(c, g, bias):
        mx = None
        for j in range(ATTN_PIECES):
            rows = slice(j * piece, (j + 1) * piece)
            s = s_ref[g, rows, :] + jnp.concatenate([bias[rows]] * Q_PER_KV, axis=1)
            s_ref[g, rows, :] = s
            pm = jnp.max(s.reshape(piece // SUBLANES, SUBLANES, wide), axis=0)
            mx = pm if mx is None else jnp.maximum(mx, pm)
        m_old = m_ref[g]
        m_new = jnp.maximum(m_old, jnp.max(mx, axis=0, keepdims=True))
        m_ref[g] = m_new
        for j in range(ATTN_PIECES):
            rows = slice(j * piece, (j + 1) * piece)
            p_ref[g, rows, :] = jnp.exp2(s_ref[g, rows, :] - m_new).astype(BF16)
        pv = jnp.dot(vta_ref[0, g, :, chunk(c)], p_ref[g], preferred_element_type=F32)
        acc_ref[g] = jnp.exp2(m_old - m_new) * acc_ref[g] + pv

    bias, ties = mask_bias(0, jnp.zeros((1, Q_BLOCK), F32))
    raw_scores(0, 0)
    for c in range(n_chunks):
        raw_scores(c, 1)
        softmax_pv(c, 0, bias)
        if c + 1 < n_chunks:
            bias_next, ties = mask_bias(c + 1, ties)
            raw_scores(c + 1, 0)
        softmax_pv(c, 1, bias)
        if c + 1 < n_chunks:
            bias = bias_next

    for g in range(N_KV_HEADS):
        acc = acc_ref[g]
        out = acc[0:HEAD_DIM, :] / acc[HEAD_DIM:HEAD_DIM + 1, :]
        for r in range(0, Q_PER_KV, 2):
            pair = jnp.concatenate([out[:, r * Q_BLOCK:(r + 1) * Q_BLOCK], out[:, (r + 1) * Q_BLOCK:(r + 2) * Q_BLOCK]],
                                   axis=0)
            col = (g * Q_PER_KV + r) * HEAD_DIM
            o_ref[0, :, col:col + 2 * HEAD_DIM] = pair.T.astype(o_ref.dtype)


def _dsa(qt, qit, wt, kib, k, vta, top_k):
    b, _, _, l = qt.shape
    assert l // PACK <= 256, "bf16 counters hold at most 256 keys per slot"
    wide = Q_PER_KV * Q_BLOCK
    per = KEY_CHUNK // Q_BLOCK
    outs = []
    for n_chunks in range(1, l // KEY_CHUNK + 1):
        first = (n_chunks - 1) * per
        keys = n_chunks * KEY_CHUNK
        outs.append(pl.pallas_call(
            functools.partial(_dsa_kernel, top_k, n_chunks),
            grid=(b, per),
            in_specs=[
                pl.BlockSpec((1, N_Q_HEADS, HEAD_DIM, Q_BLOCK), lambda bi, i, first=first: (bi, 0, 0, first + i)),
                pl.BlockSpec((1, IDX_HEADS, IDX_DIM, Q_BLOCK), lambda bi, i, first=first: (bi, 0, 0, first + i)),
                pl.BlockSpec((1, SUBLANES, Q_BLOCK), lambda bi, i, first=first: (bi, 0, first + i)),
                pl.BlockSpec((1, keys, LANES), lambda bi, i: (bi, 0, 0)),
                pl.BlockSpec((1, keys, LANES), lambda bi, i: (bi, 0, 0)),
                pl.BlockSpec((1, N_KV_HEADS, V_ROWS, keys), lambda bi, i: (bi, 0, 0, 0)),
            ],
            out_specs=pl.BlockSpec((1, Q_BLOCK, ATTN_WIDTH), lambda bi, i: (bi, i, 0)),
            out_shape=jax.ShapeDtypeStruct((b, KEY_CHUNK, ATTN_WIDTH), BF16),
            scratch_shapes=[
                pltpu.VMEM((keys, Q_BLOCK), F32),
                pltpu.VMEM((keys // PACK, PACK, Q_BLOCK), BF16),
                pltpu.VMEM((N_KV_HEADS, KEY_CHUNK, wide), F32),
                pltpu.VMEM((N_KV_HEADS, KEY_CHUNK, wide), BF16),
                pltpu.VMEM((N_KV_HEADS, 1, wide), F32),
                pltpu.VMEM((N_KV_HEADS, V_ROWS, wide), F32),
            ],
            compiler_params=pltpu.CompilerParams(dimension_semantics=("arbitrary", "arbitrary"),
                                                 vmem_limit_bytes=VMEM_LIMIT_BYTES),
            name=f"dsa{n_chunks}",
        )(qt, qit, wt, kib, k, vta))
    return jnp.concatenate(outs, axis=1)


def _rglru_kernel(rx_ref, rgate_ref, cw_ref, cb_ref, wg_ref, ba_ref, bx_ref, lam_ref, o_ref, xbuf_ref, h_ref):
    rows = rx_ref.shape[1]

    @pl.when(pl.program_id(1) == 0)
    def _():
        xbuf_ref[...] = jnp.zeros(xbuf_ref.shape, F32)
        h_ref[...] = jnp.zeros(h_ref.shape, F32)

    x = rx_ref[0].astype(F32)
    groups = rows // SUBLANES
    sub = lax.broadcasted_iota(I32, (groups, SUBLANES, LANES), 1)
    sub1 = sub[0]

    for n in range(RNN_WIDTH // LANES):
        sl = slice(n * LANES, (n + 1) * LANES)
        xs = x[:, sl]
        cw = cw_ref[:, sl]
        cb = cb_ref[:, sl]
        taps = lambda x3, x2, x1, x0: cb + cw[0:1] * x3 + cw[1:2] * x2 + cw[2:3] * x1 + cw[3:4] * x0
        y = taps(pltpu.roll(xs, 3, axis=0), pltpu.roll(xs, 2, axis=0), pltpu.roll(xs, 1, axis=0), xs)
        tail = xbuf_ref[:, sl]
        head = xs[0:SUBLANES]
        back = lambda k: jnp.where(sub1 < k, pltpu.roll(tail, k, axis=0), pltpu.roll(head, k, axis=0))
        xc = jnp.concatenate([taps(back(3), back(2), back(1), head), y[SUBLANES:]], axis=0)
        gates = jnp.dot(xc.astype(BF16), wg_ref[n], preferred_element_type=F32)
        r = _sigmoid(gates[:, 0:LANES] + ba_ref[:, sl])
        i = _sigmoid(gates[:, LANES:2 * LANES] + bx_ref[:, sl])
        lam = lam_ref[:, sl]
        log_sig = jnp.minimum(lam, 0.0) - jnp.log1p(jnp.exp(-jnp.abs(lam)))
        log_a = LRU_C * r * log_sig
        a = jnp.exp(log_a)
        b = jnp.sqrt(1.0 - a * a) * (i * xc)
        a3 = a.reshape(groups, SUBLANES, LANES)
        b3 = b.reshape(groups, SUBLANES, LANES)
        d = 1
        while d < SUBLANES:
            live = sub >= d
            a_prev = jnp.where(live, pltpu.roll(a3, d, axis=1), 1.0)
            b_prev = jnp.where(live, pltpu.roll(b3, d, axis=1), 0.0)
            b3 = a3 * b_prev + b3
            a3 = a3 * a_prev
            d *= 2
        h_prev = h_ref[:, sl]
        hs = []
        for t in range(groups):
            h_t = a3[t] * h_prev + b3[t]
            hs.append(h_t)
            h_prev = h_t[SUBLANES - 1:SUBLANES, :]
        h_ref[:, sl] = h_prev
        h = jnp.concatenate(hs, axis=0)
        o_ref[0, :, sl] = (h * _gelu_tanh(rgate_ref[0, :, sl].astype(F32))).astype(o_ref.dtype)

    xbuf_ref[...] = x[rows - SUBLANES:rows, :]


def _rglru(rx, rgate, cw, cb, wg, ba, bx, lam):
    b, l, _ = rx.shape
    rows = RNN_ROWS
    blk = lambda bi, i: (bi, i, 0)
    c2 = lambda bi, i: (0, 0)
    c3 = lambda bi, i: (0, 0, 0)
    return pl.pallas_call(
        _rglru_kernel,
        grid=(b, l // rows),
        in_specs=[
            pl.BlockSpec((1, rows, RNN_WIDTH), blk),
            pl.BlockSpec((1, rows, RNN_WIDTH), blk),
            pl.BlockSpec((CONV_WIDTH, RNN_WIDTH), c2),
            pl.BlockSpec((1, RNN_WIDTH), c2),
            pl.BlockSpec((RNN_WIDTH // LANES, LANES, 2 * LANES), c3),
            pl.BlockSpec((1, RNN_WIDTH), c2),
            pl.BlockSpec((1, RNN_WIDTH), c2),
            pl.BlockSpec((1, RNN_WIDTH), c2),
        ],
        out_specs=pl.BlockSpec((1, rows, RNN_WIDTH), blk),
        out_shape=jax.ShapeDtypeStruct((b, l, RNN_WIDTH), BF16),
        scratch_shapes=[
            pltpu.VMEM((SUBLANES, RNN_WIDTH), F32),
            pltpu.VMEM((1, RNN_WIDTH), F32),
        ],
        compiler_params=pltpu.CompilerParams(dimension_semantics=("arbitrary", "arbitrary"),
                                             vmem_limit_bytes=VMEM_LIMIT_BYTES),
        name="rglru",
    )(rx, rgate, cw, cb, wg, ba, bx, lam)


def _merge_kernel(attn_ref, rnn_ref, ga_ref, gb_ref, x_ref, woa_ref, wor_ref, wout_ref, g2_ref, x1_ref, xn_ref):
    a = jnp.dot(attn_ref[...], woa_ref[...], preferred_element_type=F32)
    r = jnp.dot(rnn_ref[...], wor_ref[...], preferred_element_type=F32)
    merged = _sigmoid(ga_ref[...].astype(F32)) * a + _sigmoid(gb_ref[...].astype(F32)) * r
    x1 = x_ref[...] + jnp.dot(merged.astype(BF16), wout_ref[...], preferred_element_type=F32)
    x1_ref[...] = x1
    ms = jnp.mean(x1 * x1, axis=-1, keepdims=True)
    xn_ref[...] = (x1 * lax.rsqrt(ms + EPS) * g2_ref[...]).astype(BF16)


def _merge(attn, rnn, ga, gb, x2, woa, wor, wout, g2):
    m = x2.shape[0]
    rows = MERGE_ROWS
    row = lambda i: (i, 0)
    const = lambda i: (0, 0)
    return pl.pallas_call(
        _merge_kernel,
        grid=(m // rows,),
        in_specs=[
            pl.BlockSpec((rows, ATTN_WIDTH), row),
            pl.BlockSpec((rows, RNN_WIDTH), row),
            pl.BlockSpec((rows, D_MODEL), row),
            pl.BlockSpec((rows, D_MODEL), row),
            pl.BlockSpec((rows, D_MODEL), row),
            pl.BlockSpec((ATTN_WIDTH, D_MODEL), const, pipeline_mode=pl.Buffered(1)),
            pl.BlockSpec((RNN_WIDTH, D_MODEL), const, pipeline_mode=pl.Buffered(1)),
            pl.BlockSpec((D_MODEL, D_MODEL), const, pipeline_mode=pl.Buffered(1)),
            pl.BlockSpec((1, D_MODEL), const),
        ],
        out_specs=[pl.BlockSpec((rows, D_MODEL), row), pl.BlockSpec((rows, D_MODEL), row)],
        out_shape=[jax.ShapeDtypeStruct((m, D_MODEL), F32), jax.ShapeDtypeStruct((m, D_MODEL), BF16)],
        compiler_params=pltpu.CompilerParams(dimension_semantics=("arbitrary",), vmem_limit_bytes=VMEM_LIMIT_BYTES),
        name="merge",
    )(attn, rnn, ga, gb, x2, woa, wor, wout, g2)


def _ffn_kernel(xn_ref, x1_ref, wup_ref, cw_ref, cb_ref, wdn_ref, o_ref, tail_ref, hid_ref):
    rows = xn_ref.shape[1]

    @pl.when(pl.program_id(1) == 0)
    def _():
        tail_ref[...] = jnp.zeros(tail_ref.shape, F32)

    xn = xn_ref[0]
    sub = lax.broadcasted_iota(I32, (SUBLANES, FFN_CHUNK), 0)

    def conv(col0):
        cols = slice(col0, col0 + FFN_CHUNK)
        u = jnp.dot(xn, wup_ref[:, cols], preferred_element_type=F32)
        cw = cw_ref[:, cols]
        cb = cb_ref[:, cols]
        taps = lambda u2, u1, u0: cb + cw[0:1] * u2 + cw[1:2] * u1 + cw[2:3] * u0
        y = taps(pltpu.roll(u, 2, axis=0), pltpu.roll(u, 1, axis=0), u)
        tail = tail_ref[:, cols]
        head = u[0:SUBLANES]
        prev1 = tail[SUBLANES - 1:SUBLANES]
        prev2 = tail[SUBLANES - 2:SUBLANES - 1]
        h1 = jnp.where(sub == 0, prev1, pltpu.roll(head, 1, axis=0))
        h2 = jnp.where(sub == 0, prev2, jnp.where(sub == 1, prev1, pltpu.roll(head, 2, axis=0)))
        tail_ref[:, cols] = u[rows - SUBLANES:rows]
        return jnp.concatenate([taps(h2, h1, head), y[SUBLANES:]], axis=0)

    for j in range(D_FF // FFN_CHUNK):
        gate = conv(j * FFN_CHUNK)
        val = conv(D_FF + j * FFN_CHUNK)
        hid_ref[:, j * FFN_CHUNK:(j + 1) * FFN_CHUNK] = (gate * _sigmoid(gate) * val).astype(BF16)
    o_ref[0] = x1_ref[0] + jnp.dot(hid_ref[...], wdn_ref[...], preferred_element_type=F32)


def _ffn(xn, x1, wup, cw, cb, wdn):
    b, l, _ = xn.shape
    rows = FFN_ROWS
    blk = lambda bi, i: (bi, i, 0)
    const = lambda bi, i: (0, 0)
    return pl.pallas_call(
        _ffn_kernel,
        grid=(b, l // rows),
        in_specs=[
            pl.BlockSpec((1, rows, D_MODEL), blk),
            pl.BlockSpec((1, rows, D_MODEL), blk),
            pl.BlockSpec((D_MODEL, 2 * D_FF), const, pipeline_mode=pl.Buffered(1)),
            pl.BlockSpec((FFN_CONV_WIDTH, 2 * D_FF), const),
            pl.BlockSpec((1, 2 * D_FF), const),
            pl.BlockSpec((D_FF, D_MODEL), const, pipeline_mode=pl.Buffered(1)),
        ],
        out_specs=pl.BlockSpec((1, rows, D_MODEL), blk),
        out_shape=jax.ShapeDtypeStruct((b, l, D_MODEL), F32),
        scratch_shapes=[
            pltpu.VMEM((SUBLANES, 2 * D_FF), F32),
            pltpu.VMEM((rows, D_FF), BF16),
        ],
        compiler_params=pltpu.CompilerParams(dimension_semantics=("arbitrary", "arbitrary"),
                                             vmem_limit_bytes=VMEM_LIMIT_BYTES),
        name="ffn",
    )(xn, x1, wup, cw, cb, wdn)


def _block_diag_pairs(w):
    z = jnp.zeros((RNN_BLOCK_DIM, RNN_BLOCK_DIM), w.dtype)
    w = w.reshape(RNN_BLOCKS // 2, 2, RNN_BLOCK_DIM, RNN_BLOCK_DIM)
    top = jnp.concatenate([w[:, 0], jnp.broadcast_to(z, w[:, 0].shape)], axis=-1)
    bot = jnp.concatenate([jnp.broadcast_to(z, w[:, 1].shape), w[:, 1]], axis=-1)
    return jnp.concatenate([top, bot], axis=-2)


def _layer(x, top_k, norm1_g, w_in, q_norm_g, k_norm_g, kidx_norm_g, conv_w, conv_b, rg_wa, rg_ba, rg_wx, rg_bx,
           rg_lambda, w_o_attn, w_o_rnn, w_out, norm2_g, w_up, ffn_conv_w, ffn_conv_b, w_down):
    b, l, _ = x.shape
    m = b * l
    x2 = x.reshape(m, D_MODEL)

    n_head = ATTN_WIDTH + 2 * KV_WIDTH + IDX_HEADS * IDX_DIM + IDX_DIM + IDX_HEADS
    pad = jnp.zeros((D_MODEL, ATTN_SLAB - n_head), w_in.dtype)
    w_head = jnp.concatenate([w_in[:, :n_head], pad], axis=1).astype(BF16)
    w_wide = w_in[:, n_head:].astype(BF16)
    pair = lambda g: jnp.tile(g, 2).reshape(1, LANES)
    kig = jnp.concatenate([kidx_norm_g, jnp.zeros((LANES - IDX_DIM,), F32)]).reshape(1, LANES)

    qt, qit, wt, kib, k, vta, rx, rgate, ga, gb = _proj(x2, norm1_g.reshape(1, D_MODEL), w_head, w_wide,
                                                          pair(q_norm_g), pair(k_norm_g), kig, b, l)
    attn = _dsa(qt, qit, wt, kib, k, vta, top_k)

    wg = jnp.concatenate([_block_diag_pairs(rg_wa), _block_diag_pairs(rg_wx)], axis=-1).astype(BF16)
    rnn = _rglru(rx.reshape(b, l, RNN_WIDTH), rgate.reshape(b, l, RNN_WIDTH), conv_w, conv_b.reshape(1, -1), wg,
                 rg_ba.reshape(1, -1), rg_bx.reshape(1, -1), rg_lambda.reshape(1, -1))

    x1, xn2 = _merge(attn.reshape(m, ATTN_WIDTH), rnn.reshape(m, RNN_WIDTH), ga, gb, x2,
                     w_o_attn.astype(BF16), w_o_rnn.astype(BF16), w_out.astype(BF16), norm2_g.reshape(1, D_MODEL))

    out = _ffn(xn2.reshape(b, l, D_MODEL), x1.reshape(b, l, D_MODEL), w_up.astype(BF16), ffn_conv_w,
               ffn_conv_b.reshape(1, -1), w_down.astype(BF16))
    return out


def kernel(x, norm1_g, w_in, q_norm_g, k_norm_g, kidx_norm_g, conv_w, conv_b, rg_wa, rg_ba, rg_wx, rg_bx, rg_lambda,
           w_o_attn, w_o_rnn, w_out, norm2_g, w_up, ffn_conv_w, ffn_conv_b, w_down):
    l = x.shape[1]
    top_k = min(TOPK_MAX, l // 4)
    h = x
    for d in range(norm1_g.shape[0]):
        h = _layer(h, top_k, norm1_g[d], w_in[d], q_norm_g[d], k_norm_g[d], kidx_norm_g[d], conv_w[d], conv_b[d],
                   rg_wa[d], rg_ba[d], rg_wx[d], rg_bx[d], rg_lambda[d], w_o_attn[d], w_o_rnn[d], w_out[d],
                   norm2_g[d], w_up[d], ffn_conv_w[d], ffn_conv_b[d], w_down[d])
    return h
```

```python
import functools

import jax
import jax.numpy as jnp
import numpy as np
from jax import lax
from jax.experimental import pallas as pl
from jax.experimental.pallas import tpu as pltpu

F32 = jnp.float32
BF16 = jnp.bfloat16
I32 = jnp.int32

D_MODEL = 1024
HEAD_DIM = 64
N_Q_HEADS = 8
N_KV_HEADS = 2
Q_PER_KV = N_Q_HEADS // N_KV_HEADS
ATTN_WIDTH = N_Q_HEADS * HEAD_DIM
KV_WIDTH = N_KV_HEADS * HEAD_DIM
IDX_HEADS = 4
IDX_DIM = 64
TOPK_MAX = 256
RNN_WIDTH = D_MODEL
RNN_BLOCKS = 16
RNN_BLOCK_DIM = RNN_WIDTH // RNN_BLOCKS
CONV_WIDTH = 4
LRU_C = 8.0
D_FF = 2816
FFN_CONV_WIDTH = 3
EPS = 1e-6

LANES = 128
SUBLANES = 8
PACK = 16
VMEM_LIMIT_BYTES = 56 * 1024 * 1024

PROJ_ROWS = 512
ATTN_SLAB = ATTN_WIDTH + 2 * KV_WIDTH + IDX_HEADS * IDX_DIM + LANES
Q_BLOCK = 256
KEY_CHUNK = 256
RNN_ROWS = 256
MERGE_ROWS = 1024
FFN_ROWS = 1024
FFN_CHUNK = 256

ATTN_PIECES = 4
ONES_ROWS = 16
V_ROWS = HEAD_DIM + ONES_ROWS

INT_MIN = -2 ** 31
LOG2E = 1.4426950408889634
MASK_BIAS = -1e30
M_INIT = -3e38


def _sigmoid(x):
    return 1.0 / (1.0 + jnp.exp(-x))


def _gelu_tanh(x):
    c = np.sqrt(2.0 / np.pi).astype(np.float32)
    return x * (0.5 * (1.0 + jnp.tanh(c * (x + 0.044715 * (x * x * x)))))


def _pair_norm(slab, gain):
    lane = lax.broadcasted_iota(I32, slab.shape, 1)
    lo = lane < HEAD_DIM
    sq = slab * slab
    s_lo = jnp.sum(jnp.where(lo, sq, 0.0), axis=-1, keepdims=True)
    s_hi = jnp.sum(jnp.where(lo, 0.0, sq), axis=-1, keepdims=True)
    inv = jnp.where(lo, lax.rsqrt(s_lo * (1.0 / HEAD_DIM) + EPS), lax.rsqrt(s_hi * (1.0 / HEAD_DIM) + EPS))
    return slab * inv * gain


def _proj_kernel(x_ref, g1_ref, wh_ref, ww_ref, qg_ref, kg_ref, kig_ref,
                 qt_ref, qit_ref, wt_ref, kib_ref, k_ref, vta_ref, rx_ref, rgate_ref, ga_ref, gb_ref):
    x = x_ref[...]
    rows = x.shape[0]
    ms = jnp.mean(x * x, axis=-1, keepdims=True)
    xn = (x * lax.rsqrt(ms + EPS) * g1_ref[...]).astype(BF16)

    pa = jnp.dot(xn, wh_ref[...], preferred_element_type=F32)
    qg = qg_ref[...]
    kg = kg_ref[...]

    def put_heads(ref, first_head, slab):
        t = slab.T
        ref[0, first_head] = t[0:HEAD_DIM].astype(ref.dtype)
        ref[0, first_head + 1] = t[HEAD_DIM:2 * HEAD_DIM].astype(ref.dtype)

    scale = (HEAD_DIM ** -0.5) * LOG2E
    for j in range(ATTN_WIDTH // LANES):
        sl = slice(j * LANES, (j + 1) * LANES)
        put_heads(qt_ref, 2 * j, _pair_norm(pa[:, sl], qg) * scale)
    off = ATTN_WIDTH
    k_ref[0] = _pair_norm(pa[:, off:off + KV_WIDTH], kg).astype(BF16)
    off += KV_WIDTH
    put_heads(vta_ref.at[:, :, 0:HEAD_DIM, :], 0, pa[:, off:off + KV_WIDTH])
    ones = jnp.ones((ONES_ROWS, rows), BF16)
    for g in range(N_KV_HEADS):
        vta_ref[0, g, HEAD_DIM:HEAD_DIM + ONES_ROWS, :] = ones
    off += KV_WIDTH
    for j in range(IDX_HEADS * IDX_DIM // LANES):
        put_heads(qit_ref, 2 * j, pa[:, off + j * LANES:off + (j + 1) * LANES] * (IDX_DIM ** -0.5))
    off += IDX_HEADS * IDX_DIM
    tail = pa[:, off:off + LANES]
    lane = lax.broadcasted_iota(I32, tail.shape, 1)
    is_ki = lane < IDX_DIM
    s_ki = jnp.sum(jnp.where(is_ki, tail * tail, 0.0), axis=-1, keepdims=True)
    ki_n = tail * lax.rsqrt(s_ki * (1.0 / IDX_DIM) + EPS) * kig_ref[...]
    kiwi = jnp.where(is_ki, ki_n, tail * (IDX_HEADS ** -0.5))
    kib_ref[0] = kiwi.astype(BF16)
    wt_ref[0] = kiwi.T[IDX_DIM:IDX_DIM + SUBLANES]

    off = 0
    for ref in (rx_ref, rgate_ref, ga_ref, gb_ref):
        ref[...] = jnp.dot(xn, ww_ref[:, off:off + D_MODEL], preferred_element_type=F32).astype(BF16)
        off += D_MODEL


def _proj(x2, g1, w_head, w_wide, qg, kg, kig, b, l):
    m = x2.shape[0]
    rows = PROJ_ROWS
    tpb = l // rows
    const = lambda i: (0, 0)
    row = lambda i: (i, 0)
    head_major = lambda i: (i // tpb, 0, 0, i % tpb)
    seq_major = lambda i: (i // tpb, i % tpb, 0)
    wide = [(D_MODEL, BF16)] * 4
    return pl.pallas_call(
        _proj_kernel,
        grid=(m // rows,),
        in_specs=[
            pl.BlockSpec((rows, D_MODEL), row),
            pl.BlockSpec((1, D_MODEL), const),
            pl.BlockSpec((D_MODEL, ATTN_SLAB), const, pipeline_mode=pl.Buffered(1)),
            pl.BlockSpec((D_MODEL, 4 * D_MODEL), const, pipeline_mode=pl.Buffered(1)),
            pl.BlockSpec((1, LANES), const),
            pl.BlockSpec((1, LANES), const),
            pl.BlockSpec((1, LANES), const),
        ],
        out_specs=[
            pl.BlockSpec((1, N_Q_HEADS, HEAD_DIM, rows), head_major),
            pl.BlockSpec((1, IDX_HEADS, IDX_DIM, rows), head_major),
            pl.BlockSpec((1, SUBLANES, rows), lambda i: (i // tpb, 0, i % tpb)),
            pl.BlockSpec((1, rows, LANES), seq_major),
            pl.BlockSpec((1, rows, LANES), seq_major),
            pl.BlockSpec((1, N_KV_HEADS, V_ROWS, rows), head_major),
        ] + [pl.BlockSpec((rows, w), row) for w, _ in wide],
        out_shape=[
            jax.ShapeDtypeStruct((b, N_Q_HEADS, HEAD_DIM, l), BF16),
            jax.ShapeDtypeStruct((b, IDX_HEADS, IDX_DIM, l), BF16),
            jax.ShapeDtypeStruct((b, SUBLANES, l), F32),
            jax.ShapeDtypeStruct((b, l, LANES), BF16),
            jax.ShapeDtypeStruct((b, l, LANES), BF16),
            jax.ShapeDtypeStruct((b, N_KV_HEADS, V_ROWS, l), BF16),
        ] + [jax.ShapeDtypeStruct((m, w), dt) for w, dt in wide],
        compiler_params=pltpu.CompilerParams(dimension_semantics=("arbitrary",), vmem_limit_bytes=VMEM_LIMIT_BYTES),
        name="proj",
    )(x2, g1, w_head, w_wide, qg, kg, kig)


def _dsa_kernel(top_k, n_chunks, qt_ref, qit_ref, wt_ref, kib_ref, k_ref, vta_ref, o_ref,
                score_ref, coarse_ref, s_ref, p_ref, m_ref, acc_ref):
    blk = (n_chunks - 1) * (KEY_CHUNK // Q_BLOCK) + pl.program_id(1)
    packed = KEY_CHUNK // PACK
    wide = Q_PER_KV * Q_BLOCK
    chunk = lambda c: slice(c * KEY_CHUNK, (c + 1) * KEY_CHUNK)

    qit = jnp.concatenate([qit_ref[0, h] for h in range(IDX_HEADS)], axis=1)
    qit = jnp.concatenate([qit, jnp.zeros_like(qit)], axis=0)
    wt = wt_ref[0]

    for c in range(n_chunks):
        d = jnp.dot(kib_ref[0, chunk(c), :], qit, preferred_element_type=F32)
        s = jnp.zeros((KEY_CHUNK, Q_BLOCK), F32)
        for h in range(IDX_HEADS):
            s = s + wt[h:h + 1, :] * jnp.maximum(d[:, h * Q_BLOCK:(h + 1) * Q_BLOCK], 0.0)
        if c == n_chunks - 1:
            q_pos = blk * Q_BLOCK + lax.broadcasted_iota(I32, (KEY_CHUNK, Q_BLOCK), 1)
            key_pos = c * KEY_CHUNK + lax.broadcasted_iota(I32, (KEY_CHUNK, Q_BLOCK), 0)
            s = jnp.where(key_pos <= q_pos, s, -jnp.inf)
        score_ref[chunk(c), :] = s
        coarse_ref[c * packed:(c + 1) * packed] = s.astype(BF16).reshape(packed, PACK, Q_BLOCK)

    n_acc = 4

    def order_flip(key):
        return key ^ ((key >> 31) & 0x7FFFFFFF)

    def as_float(key):
        return lax.bitcast_convert_type(order_flip(key), F32)

    one = jnp.ones((PACK, Q_BLOCK), BF16)
    zero = jnp.zeros((PACK, Q_BLOCK), BF16)

    def count_coarse(level):
        lvl = jnp.broadcast_to(level.astype(BF16), (PACK, Q_BLOCK))
        accs = [zero] * n_acc
        for i in range(n_chunks * packed):
            accs[i % n_acc] = accs[i % n_acc] + jnp.where(coarse_ref[i] >= lvl, one, zero)
        tot = (accs[0].astype(F32) + accs[1].astype(F32)) + (accs[2].astype(F32) + accs[3].astype(F32))
        return jnp.sum(tot, axis=0, keepdims=True)

    def count(compare, level):
        lvl = jnp.broadcast_to(level, (SUBLANES, Q_BLOCK))
        accs = [jnp.zeros((SUBLANES, Q_BLOCK), I32)] * n_acc
        for i in range(n_chunks * KEY_CHUNK // SUBLANES):
            hit = compare(score_ref[i * SUBLANES:(i + 1) * SUBLANES, :], lvl)
            accs[i % n_acc] = accs[i % n_acc] + jnp.where(hit, 1, 0)
        return jnp.sum((accs[0] + accs[1]) + (accs[2] + accs[3]), axis=0, keepdims=True)

    def coarse_body(it, cand):
        trial = cand + jnp.left_shift(jnp.int32(1), 31 - it)
        level = lax.bitcast_convert_type(order_flip(trial) & -65536, F32)
        return jnp.where(count_coarse(level) >= top_k, trial, cand)

    coarse = lax.fori_loop(0, 16, coarse_body, jnp.full((1, Q_BLOCK), INT_MIN, I32))
    centre = order_flip(order_flip(coarse) & -65536)
    base = centre - 32768

    def fine_body(it, offset):
        trial = offset + jnp.left_shift(jnp.int32(1), 16 - it)
        return jnp.where(count(jnp.greater_equal, as_float(base + trial)) >= top_k, trial, offset)

    thr = as_float(base + lax.fori_loop(0, 17, fine_body, jnp.zeros((1, Q_BLOCK), I32)))
    thr = jnp.where(coarse == INT_MIN, -jnp.inf, thr)
    need = jnp.where(thr == -jnp.inf, 0, top_k - count(jnp.greater, thr)).astype(F32)

    zeros_q = jnp.zeros((HEAD_DIM, wide), BF16)
    qt_groups = []
    for g in range(N_KV_HEADS):
        qg = jnp.concatenate([qt_ref[0, g * Q_PER_KV + r] for r in range(Q_PER_KV)], axis=1)
        qt_groups.append(jnp.concatenate([qg, zeros_q] if g == 0 else [zeros_q, qg], axis=0))
    half = KEY_CHUNK // 2
    tri_r = lax.broadcasted_iota(I32, (KEY_CHUNK, KEY_CHUNK), 0)
    tri_c = lax.broadcasted_iota(I32, (KEY_CHUNK, KEY_CHUNK), 1)
    prefix_mat = jnp.where(tri_c <= tri_r, 1.0, 0.0).astype(BF16)

    m_ref[...] = jnp.full(m_ref.shape, M_INIT, F32)
    acc_ref[...] = jnp.zeros(acc_ref.shape, F32)
    piece = KEY_CHUNK // ATTN_PIECES

    def mask_bias(c, ties_before):
        kc = score_ref[chunk(c), :]
        is_tie = kc == thr
        tie = jnp.where(is_tie, 1.0, 0.0).astype(BF16)
        rank_top = jnp.dot(prefix_mat[0:half, 0:half], tie[0:half], preferred_element_type=F32)
        rank_bot = jnp.dot(prefix_mat[half:KEY_CHUNK, :], tie, preferred_element_type=F32)
        tie_rank = ties_before + jnp.concatenate([rank_top, rank_bot], axis=0)
        admitted = jnp.where(is_tie, tie_rank, 2.0 * KEY_CHUNK * KEY_CHUNK) <= need
        bias = jnp.where(kc > thr, 0.0, jnp.where(admitted, 0.0, MASK_BIAS))
        return bias, tie_rank[KEY_CHUNK - 1:KEY_CHUNK, :]

    def raw_scores(c, g):
        s_ref[g] = jnp.dot(k_ref[0, chunk(c), :], qt_groups[g], preferred_element_type=F32)

    def softmax_pv(c, g, bias):
        mx = None
        for j in range(ATTN_PIECES):
            rows = slice(j * piece, (j + 1) * piece)
            s = s_ref[g, rows, :] + jnp.concatenate([bias[rows]] * Q_PER_KV, axis=1)
            s_ref[g, rows, :] = s
            pm = jnp.max(s.reshape(piece // SUBLANES, SUBLANES, wide), axis=0)
            mx = pm if mx is None else jnp.maximum(mx, pm)
        m_old = m_ref[g]
        m_new = jnp.maximum(m_old, jnp.max(mx, axis=0, keepdims=True))
        m_ref[g] = m_new
        for j in range(ATTN_PIECES):
            rows = slice(j * piece, (j + 1) * piece)
            p_ref[g, rows, :] = jnp.exp2(s_ref[g, rows, :] - m_new).astype(BF16)
        pv = jnp.dot(vta_ref[0, g, :, chunk(c)], p_ref[g], preferred_element_type=F32)
        acc_ref[g] = jnp.exp2(m_old - m_new) * acc_ref[g] + pv

    bias, ties = mask_bias(0, jnp.zeros((1, Q_BLOCK), F32))
    raw_scores(0, 0)
    for c in range(n_chunks):
        raw_scores(c, 1)
        softmax_pv(c, 0, bias)
        if c + 1 < n_chunks:
            bias_next, ties = mask_bias(c + 1, ties)
            raw_scores(c + 1, 0)
        softmax_pv(c, 1, bias)
        if c + 1 < n_chunks:
            bias = bias_next

    for g in range(N_KV_HEADS):
        acc = acc_ref[g]
        out = acc[0:HEAD_DIM, :] / acc[HEAD_DIM:HEAD_DIM + 1, :]
        for r in range(0, Q_PER_KV, 2):
            pair = jnp.concatenate([out[:, r * Q_BLOCK:(r + 1) * Q_BLOCK], out[:, (r + 1) * Q_BLOCK:(r + 2) * Q_BLOCK]],
                                   axis=0)
            col = (g * Q_PER_KV + r) * HEAD_DIM
            o_ref[0, :, col:col + 2 * HEAD_DIM] = pair.T.astype(o_ref.dtype)


def _dsa(qt, qit, wt, kib, k, vta, top_k):
    b, _, _, l = qt.shape
    assert l // PACK <= 256, "bf16 counters hold at most 256 keys per slot"
    wide = Q_PER_KV * Q_BLOCK
    per = KEY_CHUNK // Q_BLOCK
    outs = []
    for n_chunks in range(1, l // KEY_CHUNK + 1):
        first = (n_chunks - 1) * per
        keys = n_chunks * KEY_CHUNK
        outs.append(pl.pallas_call(
            functools.partial(_dsa_kernel, top_k, n_chunks),
            grid=(b, per),
            in_specs=[
                pl.BlockSpec((1, N_Q_HEADS, HEAD_DIM, Q_BLOCK), lambda bi, i, first=first: (bi, 0, 0, first + i)),
                pl.BlockSpec((1, IDX_HEADS, IDX_DIM, Q_BLOCK), lambda bi, i, first=first: (bi, 0, 0, first + i)),
                pl.BlockSpec((1, SUBLANES, Q_BLOCK), lambda bi, i, first=first: (bi, 0, first + i)),
                pl.BlockSpec((1, keys, LANES), lambda bi, i: (bi, 0, 0)),
                pl.BlockSpec((1, keys, LANES), lambda bi, i: (bi, 0, 0)),
                pl.BlockSpec((1, N_KV_HEADS, V_ROWS, keys), lambda bi, i: (bi, 0, 0, 0)),
            ],
            out_specs=pl.BlockSpec((1, Q_BLOCK, ATTN_WIDTH), lambda bi, i: (bi, i, 0)),
            out_shape=jax.ShapeDtypeStruct((b, KEY_CHUNK, ATTN_WIDTH), BF16),
            scratch_shapes=[
                pltpu.VMEM((keys, Q_BLOCK), F32),
                pltpu.VMEM((keys // PACK, PACK, Q_BLOCK), BF16),
                pltpu.VMEM((N_KV_HEADS, KEY_CHUNK, wide), F32),
                pltpu.VMEM((N_KV_HEADS, KEY_CHUNK, wide), BF16),
                pltpu.VMEM((N_KV_HEADS, 1, wide), F32),
                pltpu.VMEM((N_KV_HEADS, V_ROWS, wide), F32),
            ],
            compiler_params=pltpu.CompilerParams(dimension_semantics=("arbitrary", "arbitrary"),
                                                 vmem_limit_bytes=VMEM_LIMIT_BYTES),
            name=f"dsa{n_chunks}",
        )(qt, qit, wt, kib, k, vta))
    return jnp.concatenate(outs, axis=1)


def _rglru_kernel(rx_ref, rgate_ref, cw_ref, cb_ref, wg_ref, ba_ref, bx_ref, lam_ref, o_ref, xbuf_ref, h_ref):
    rows = rx_ref.shape[1]

    @pl.when(pl.program_id(1) == 0)
    def _():
        xbuf_ref[...] = jnp.zeros(xbuf_ref.shape, F32)
        h_ref[...] = jnp.zeros(h_ref.shape, F32)

    x = rx_ref[0].astype(F32)
    groups = rows // SUBLANES
    sub = lax.broadcasted_iota(I32, (groups, SUBLANES, LANES), 1)
    sub1 = sub[0]

    for n in range(RNN_WIDTH // LANES):
        sl = slice(n * LANES, (n + 1) * LANES)
        xs = x[:, sl]
        cw = cw_ref[:, sl]
        cb = cb_ref[:, sl]
        taps = lambda x3, x2, x1, x0: cb + cw[0:1] * x3 + cw[1:2] * x2 + cw[2:3] * x1 + cw[3:4] * x0
        y = taps(pltpu.roll(xs, 3, axis=0), pltpu.roll(xs, 2, axis=0), pltpu.roll(xs, 1, axis=0), xs)
        tail = xbuf_ref[:, sl]
        head = xs[0:SUBLANES]
        back = lambda k: jnp.where(sub1 < k, pltpu.roll(tail, k, axis=0), pltpu.roll(head, k, axis=0))
        xc = jnp.concatenate([taps(back(3), back(2), back(1), head), y[SUBLANES:]], axis=0)
        gates = jnp.dot(xc.astype(BF16), wg_ref[n], preferred_element_type=F32)
        r = _sigmoid(gates[:, 0:LANES] + ba_ref[:, sl])
        i = _sigmoid(gates[:, LANES:2 * LANES] + bx_ref[:, sl])
        lam = lam_ref[:, sl]
        log_sig = jnp.minimum(lam, 0.0) - jnp.log1p(jnp.exp(-jnp.abs(lam)))
        log_a = LRU_C * r * log_sig
        a = jnp.exp(log_a)
        b = jnp.sqrt(1.0 - a * a) * (i * xc)
        a3 = a.reshape(groups, SUBLANES, LANES)
        b3 = b.reshape(groups, SUBLANES, LANES)
        d = 1
        while d < SUBLANES:
            live = sub >= d
            a_prev = jnp.where(live, pltpu.roll(a3, d, axis=1), 1.0)
            b_prev = jnp.where(live, pltpu.roll(b3, d, axis=1), 0.0)
            b3 = a3 * b_prev + b3
            a3 = a3 * a_prev
            d *= 2
        h_prev = h_ref[:, sl]
        hs = []
        for t in range(groups):
            h_t = a3[t] * h_prev + b3[t]
            hs.append(h_t)
            h_prev = h_t[SUBLANES - 1:SUBLANES, :]
        h_ref[:, sl] = h_prev
        h = jnp.concatenate(hs, axis=0)
        o_ref[0, :, sl] = (h * _gelu_tanh(rgate_ref[0, :, sl].astype(F32))).astype(o_ref.dtype)

    xbuf_ref[...] = x[rows - SUBLANES:rows, :]


def _rglru(rx, rgate, cw, cb, wg, ba, bx, lam):
    b, l, _ = rx.shape
    rows = RNN_ROWS
    blk = lambda bi, i: (bi, i, 0)
    c2 = lambda bi, i: (0, 0)
    c3 = lambda bi, i: (0, 0, 0)
    return pl.pallas_call(
        _rglru_kernel,
        grid=(b, l // rows),
        in_specs=[
            pl.BlockSpec((1, rows, RNN_WIDTH), blk),
            pl.BlockSpec((1, rows, RNN_WIDTH), blk),
            pl.BlockSpec((CONV_WIDTH, RNN_WIDTH), c2),
            pl.BlockSpec((1, RNN_WIDTH), c2),
            pl.BlockSpec((RNN_WIDTH // LANES, LANES, 2 * LANES), c3),
            pl.BlockSpec((1, RNN_WIDTH), c2),
            pl.BlockSpec((1, RNN_WIDTH), c2),
            pl.BlockSpec((1, RNN_WIDTH), c2),
        ],
        out_specs=pl.BlockSpec((1, rows, RNN_WIDTH), blk),
        out_shape=jax.ShapeDtypeStruct((b, l, RNN_WIDTH), BF16),
        scratch_shapes=[
            pltpu.VMEM((SUBLANES, RNN_WIDTH), F32),
            pltpu.VMEM((1, RNN_WIDTH), F32),
        ],
        compiler_params=pltpu.CompilerParams(dimension_semantics=("arbitrary", "arbitrary"),
                                             vmem_limit_bytes=VMEM_LIMIT_BYTES),
        name="rglru",
    )(rx, rgate, cw, cb, wg, ba, bx, lam)


def _merge_kernel(attn_ref, rnn_ref, ga_ref, gb_ref, x_ref, woa_ref, wor_ref, wout_ref, g2_ref, x1_ref, xn_ref):
    a = jnp.dot(attn_ref[...], woa_ref[...], preferred_element_type=F32)
    r = jnp.dot(rnn_ref[...], wor_ref[...], preferred_element_type=F32)
    merged = _sigmoid(ga_ref[...].astype(F32)) * a + _sigmoid(gb_ref[...].astype(F32)) * r
    x1 = x_ref[...] + jnp.dot(merged.astype(BF16), wout_ref[...], preferred_element_type=F32)
    x1_ref[...] = x1
    ms = jnp.mean(x1 * x1, axis=-1, keepdims=True)
    xn_ref[...] = (x1 * lax.rsqrt(ms + EPS) * g2_ref[...]).astype(BF16)


def _merge(attn, rnn, ga, gb, x2, woa, wor, wout, g2):
    m = x2.shape[0]
    rows = MERGE_ROWS
    row = lambda i: (i, 0)
    const = lambda i: (0, 0)
    return pl.pallas_call(
        _merge_kernel,
        grid=(m // rows,),
        in_specs=[
            pl.BlockSpec((rows, ATTN_WIDTH), row),
            pl.BlockSpec((rows, RNN_WIDTH), row),
            pl.BlockSpec((rows, D_MODEL), row),
            pl.BlockSpec((rows, D_MODEL), row),
            pl.BlockSpec((rows, D_MODEL), row),
            pl.BlockSpec((ATTN_WIDTH, D_MODEL), const, pipeline_mode=pl.Buffered(1)),
            pl.BlockSpec((RNN_WIDTH, D_MODEL), const, pipeline_mode=pl.Buffered(1)),
            pl.BlockSpec((D_MODEL, D_MODEL), const, pipeline_mode=pl.Buffered(1)),
            pl.BlockSpec((1, D_MODEL), const),
        ],
        out_specs=[pl.BlockSpec((rows, D_MODEL), row), pl.BlockSpec((rows, D_MODEL), row)],
        out_shape=[jax.ShapeDtypeStruct((m, D_MODEL), F32), jax.ShapeDtypeStruct((m, D_MODEL), BF16)],
        compiler_params=pltpu.CompilerParams(dimension_semantics=("arbitrary",), vmem_limit_bytes=VMEM_LIMIT_BYTES),
        name="merge",
    )(attn, rnn, ga, gb, x2, woa, wor, wout, g2)


def _ffn_kernel(xn_ref, x1_ref, wup_ref, cw_ref, cb_ref, wdn_ref, o_ref, tail_ref, hid_ref):
    rows = xn_ref.shape[1]

    @pl.when(pl.program_id(1) == 0)
    def _():
        tail_ref[...] = jnp.zeros(tail_ref.shape, F32)

    xn = xn_ref[0]
    sub = lax.broadcasted_iota(I32, (SUBLANES, FFN_CHUNK), 0)

    def conv(col0):
        cols = slice(col0, col0 + FFN_CHUNK)
        u = jnp.dot(xn, wup_ref[:, cols], preferred_element_type=F32)
        cw = cw_ref[:, cols]
        cb = cb_ref[:, cols]
        taps = lambda u2, u1, u0: cb + cw[0:1] * u2 + cw[1:2] * u1 + cw[2:3] * u0
        y = taps(pltpu.roll(u, 2, axis=0), pltpu.roll(u, 1, axis=0), u)
        tail = tail_ref[:, cols]
        head = u[0:SUBLANES]
        prev1 = tail[SUBLANES - 1:SUBLANES]
        prev2 = tail[SUBLANES - 2:SUBLANES - 1]
        h1 = jnp.where(sub == 0, prev1, pltpu.roll(head, 1, axis=0))
        h2 = jnp.where(sub == 0, prev2, jnp.where(sub == 1, prev1, pltpu.roll(head, 2, axis=0)))
        tail_ref[:, cols] = u[rows - SUBLANES:rows]
        return jnp.concatenate([taps(h2, h1, head), y[SUBLANES:]], axis=0)

    for j in range(D_FF // FFN_CHUNK):
        gate = conv(j * FFN_CHUNK)
        val = conv(D_FF + j * FFN_CHUNK)
        hid_ref[:, j * FFN_CHUNK:(j + 1) * FFN_CHUNK] = (gate * _sigmoid(gate) * val).astype(BF16)
    o_ref[0] = x1_ref[0] + jnp.dot(hid_ref[...], wdn_ref[...], preferred_element_type=F32)


def _ffn(xn, x1, wup, cw, cb, wdn):
    b, l, _ = xn.shape
    rows = FFN_ROWS
    blk = lambda bi, i: (bi, i, 0)
    const = lambda bi, i: (0, 0)
    return pl.pallas_call(
        _ffn_kernel,
        grid=(b, l // rows),
        in_specs=[
            pl.BlockSpec((1, rows, D_MODEL), blk),
            pl.BlockSpec((1, rows, D_MODEL), blk),
            pl.BlockSpec((D_MODEL, 2 * D_FF), const, pipeline_mode=pl.Buffered(1)),
            pl.BlockSpec((FFN_CONV_WIDTH, 2 * D_FF), const),
            pl.BlockSpec((1, 2 * D_FF), const),
            pl.BlockSpec((D_FF, D_MODEL), const, pipeline_mode=pl.Buffered(1)),
        ],
        out_specs=pl.BlockSpec((1, rows, D_MODEL), blk),
        out_shape=jax.ShapeDtypeStruct((b, l, D_MODEL), F32),
        scratch_shapes=[
            pltpu.VMEM((SUBLANES, 2 * D_FF), F32),
            pltpu.VMEM((rows, D_FF), BF16),
        ],
        compiler_params=pltpu.CompilerParams(dimension_semantics=("arbitrary", "arbitrary"),
                                             vmem_limit_bytes=VMEM_LIMIT_BYTES),
        name="ffn",
    )(xn, x1, wup, cw, cb, wdn)


def _block_diag_pairs(w):
    z = jnp.zeros((RNN_BLOCK_DIM, RNN_BLOCK_DIM), w.dtype)
    w = w.reshape(RNN_BLOCKS // 2, 2, RNN_BLOCK_DIM, RNN_BLOCK_DIM)
    top = jnp.concatenate([w[:, 0], jnp.broadcast_to(z, w[:, 0].shape)], axis=-1)
    bot = jnp.concatenate([jnp.broadcast_to(z, w[:, 1].shape), w[:, 1]], axis=-1)
    return jnp.concatenate([top, bot], axis=-2)


def _layer(x, top_k, norm1_g, w_in, q_norm_g, k_norm_g, kidx_norm_g, conv_w, conv_b, rg_wa, rg_ba, rg_wx, rg_bx,
           rg_lambda, w_o_attn, w_o_rnn, w_out, norm2_g, w_up, ffn_conv_w, ffn_conv_b, w_down):
    b, l, _ = x.shape
    m = b * l
    x2 = x.reshape(m, D_MODEL)

    n_head = ATTN_WIDTH + 2 * KV_WIDTH + IDX_HEADS * IDX_DIM + IDX_DIM + IDX_HEADS
    pad = jnp.zeros((D_MODEL, ATTN_SLAB - n_head), w_in.dtype)
    w_head = jnp.concatenate([w_in[:, :n_head], pad], axis=1).astype(BF16)
    w_wide = w_in[:, n_head:].astype(BF16)
    pair = lambda g: jnp.tile(g, 2).reshape(1, LANES)
    kig = jnp.concatenate([kidx_norm_g, jnp.zeros((LANES - IDX_DIM,), F32)]).reshape(1, LANES)

    qt, qit, wt, kib, k, vta, rx, rgate, ga, gb = _proj(x2, norm1_g.reshape(1, D_MODEL), w_head, w_wide,
                                                          pair(q_norm_g), pair(k_norm_g), kig, b, l)
    attn = _dsa(qt, qit, wt, kib, k, vta, top_k)

    wg = jnp.concatenate([_block_diag_pairs(rg_wa), _block_diag_pairs(rg_wx)], axis=-1).astype(BF16)
    rnn = _rglru(rx.reshape(b, l, RNN_WIDTH), rgate.reshape(b, l, RNN_WIDTH), conv_w, conv_b.reshape(1, -1), wg,
                 rg_ba.reshape(1, -1), rg_bx.reshape(1, -1), rg_lambda.reshape(1, -1))

    x1, xn2 = _merge(attn.reshape(m, ATTN_WIDTH), rnn.reshape(m, RNN_WIDTH), ga, gb, x2,
                     w_o_attn.astype(BF16), w_o_rnn.astype(BF16), w_out.astype(BF16), norm2_g.reshape(1, D_MODEL))

    out = _ffn(xn2.reshape(b, l, D_MODEL), x1.reshape(b, l, D_MODEL), w_up.astype(BF16), ffn_conv_w,
               ffn_conv_b.reshape(1, -1), w_down.astype(BF16))
    return out


def kernel(x, norm1_g, w_in, q_norm_g, k_norm_g, kidx_norm_g, conv_w, conv_b, rg_wa, rg_ba, rg_wx, rg_bx, rg_lambda,
           w_o_attn, w_o_rnn, w_out, norm2_g, w_up, ffn_conv_w, ffn_conv_b, w_down):
    l = x.shape[1]
    top_k = min(TOPK_MAX, l // 4)
    h = x
    for d in range(norm1_g.shape[0]):
        h = _layer(h, top_k, norm1_g[d], w_in[d], q_norm_g[d], k_norm_g[d], kidx_norm_g[d], conv_w[d], conv_b[d],
                   rg_wa[d], rg_ba[d], rg_wx[d], rg_bx[d], rg_lambda[d], w_o_attn[d], w_o_rnn[d], w_out[d],
                   norm2_g[d], w_up[d], ffn_conv_w[d], ffn_conv_b[d], w_down[d])
    return h
```
